```python
import jax, jax.numpy as jnp
from jax import lax
import numpy as np

D_MODEL = 1024
BATCH = 32
SEQ = 2048
DEPTH = 4

GRID_W = 64
CTX_LEN = 256
N_MIXERS = 2
HEAD_SIZE = 64
N_HEADS = D_MODEL // HEAD_SIZE
DECAY_LORA = 64
ICLR_LORA = 64
VRES_LORA = 32
GATE_LORA = 128
GN_EPS = 64e-5
N_DIR = 2
CHUNK = 128
SG_GROUPS = 16
SG_FFN = 4 * D_MODEL
SG_HALF = SG_FFN // 2
N_EXPERTS = 16
N_EXPERT_GROUPS = 4
TOP_K = 2
EXPERT_FFN = 512
N_MOD = 6
NORM_EPS = 1e-6
LN_EPS = 1e-5

kernel_name = "hybrid_rwkv7_chunksgu_groupmoe_prefix"


def rms_norm(x, g):
    xf = x.astype(jnp.float32)
    y = xf * lax.rsqrt(jnp.mean(xf * xf, axis=-1, keepdims=True) + NORM_EPS)
    return (y * g.astype(jnp.float32)).astype(x.dtype)


def layer_norm(x, g, b):
    xf = x.astype(jnp.float32)
    mu = jnp.mean(xf, axis=-1, keepdims=True)
    var = jnp.mean(jnp.square(xf - mu), axis=-1, keepdims=True)
    return ((xf - mu) * lax.rsqrt(var + LN_EPS) * g.astype(jnp.float32) + b.astype(jnp.float32)).astype(x.dtype)


def modulation(cvec, w, b):
    m = jax.nn.silu(cvec) @ w + b
    m = m.reshape(-1, 1, N_MOD, m.shape[-1] // N_MOD)
    return tuple(m[:, :, j, :] for j in range(N_MOD))


def ada_norm(x, g, shift, scale):
    return rms_norm(x, g) * (1 + scale) + shift


def shift_1d(x):
    half = x.shape[-1] // 2
    prev = jnp.pad(x[:, :-1, :half], ((0, 0), (1, 0), (0, 0)))
    nxt = jnp.pad(x[:, 1:, half:], ((0, 0), (0, 1), (0, 0)))
    return jnp.concatenate([prev, nxt], axis=-1)


def shift_grid(x):
    B, L, D = x.shape
    rows = L // GRID_W
    q = D // 4
    g = x.reshape(B, rows, GRID_W, D)
    left = jnp.pad(g[:, :, :-1, :q], ((0, 0), (0, 0), (1, 0), (0, 0)))
    right = jnp.pad(g[:, :, 1:, q:2 * q], ((0, 0), (0, 0), (0, 1), (0, 0)))
    up = jnp.pad(g[:, :-1, :, 2 * q:3 * q], ((0, 0), (1, 0), (0, 0), (0, 0)))
    down = jnp.pad(g[:, 1:, :, 3 * q:], ((0, 0), (0, 1), (0, 0), (0, 0)))
    return jnp.concatenate([left, right, up, down], axis=-1).reshape(B, L, D)


def wkv_scan(decay, k, v, kk, a, r, s0, reverse):
    seq = [decay, k, v, kk, a] + ([] if r is None else [r])
    xs = tuple(jnp.moveaxis(t, 1, 0) for t in seq)

    def step(s, inp):
        w_t, k_t, v_t, kk_t, a_t = inp[:5]
        sa = jnp.einsum('bhvk,bhk->bhv', s, kk_t)
        s = (s * w_t[:, :, None, :]
             - sa[..., None] * (kk_t * a_t)[:, :, None, :]
             + v_t[..., None] * k_t[:, :, None, :])
        y = None if r is None else jnp.einsum('bhvk,bhk->bhv', s, inp[5])
        return s, y

    s_fin, ys = lax.scan(step, s0, xs, reverse=reverse)
    return (None if r is None else jnp.moveaxis(ys, 0, 1)), s_fin


def rwkv7_bidir(h_ctx, h_lat, v_first, ctx_out, mu, w_r, w_k, w_v, w_o, w0, w1, w2, a0, a1, a2, vres,
                g1, g2, k_k, k_a, r_k, lnx_g, lnx_b):
    B, Lc, D = h_ctx.shape
    H, N = N_HEADS, HEAD_SIZE
    lo = 0 if ctx_out else Lc
    h = jnp.concatenate([h_ctx, h_lat], axis=1)
    xx = jnp.concatenate([shift_1d(h_ctx), shift_grid(h_lat)], axis=1) - h
    xw, xk, xv, xa = (h + xx * mu[j] for j in (1, 2, 3, 4))
    xr = h[:, lo:] + xx[:, lo:] * mu[0]
    xg = h[:, lo:] + xx[:, lo:] * mu[5]
    k = xk @ w_k
    v = xv @ w_v
    if v_first is None:
        v_first = v
    else:
        v0, v1, v2 = vres
        v = v + (v_first - v) * jax.nn.sigmoid(v0 + (xv @ v1) @ v2)
    r = xr @ w_r
    g = jax.nn.sigmoid(xg @ g1) @ g2

    heads = lambda t: t.reshape(B, -1, H, N).astype(jnp.float32)
    r_h, v_h, k_h = heads(r), heads(v), heads(k)
    kk = heads(k * k_k)
    kk = kk / jnp.maximum(jnp.sqrt(jnp.sum(kk * kk, axis=-1, keepdims=True)), 1e-12)
    k_a_h = k_a.reshape(H, N).astype(jnp.float32)
    r_k_f = r_k.astype(jnp.float32)
    s_zero = jnp.zeros((B, H, N, N), jnp.float32)
    rc = r_h[:, :Lc] if ctx_out else None
    rl = r_h[:, Lc - lo:]

    y = None
    bonus = None
    for d in range(N_DIR):
        w_log = -jax.nn.softplus(-(w0[d] + jnp.tanh(xw @ w1[d]) @ w2[d])) - 0.5
        decay = jnp.exp(-jnp.exp(heads(w_log)))
        a = heads(jax.nn.sigmoid(a0[d] + (xa @ a1[d]) @ a2[d]))
        k_d = k_h * (1 + (a - 1) * k_a_h)
        rev = d == 1
        yc, s_ctx = wkv_scan(decay[:, :Lc], k_d[:, :Lc], v_h[:, :Lc], kk[:, :Lc], a[:, :Lc], rc, s_zero, rev)
        yl, _ = wkv_scan(decay[:, Lc:], k_d[:, Lc:], v_h[:, Lc:], kk[:, Lc:], a[:, Lc:], rl, s_ctx, rev)
        y_d = jnp.concatenate([yc, yl], axis=1) if ctx_out else yl
        b_d = jnp.sum(r_h * k_d[:, lo:] * r_k_f, axis=-1, keepdims=True) * v_h[:, lo:]
        y = y_d if y is None else y + y_d
        bonus = b_d if bonus is None else bonus + b_d

    mean = jnp.mean(y, axis=-1, keepdims=True)
    var = jnp.mean(jnp.square(y - mean), axis=-1, keepdims=True)
    y = ((y - mean) * lax.rsqrt(var + GN_EPS)).reshape(B, -1, D) * lnx_g + lnx_b + bonus.reshape(B, -1, D)
    out = (y.astype(h.dtype) * g) @ w_o
    if ctx_out:
        return out[:, :Lc], out[:, Lc:], v_first
    return None, out, v_first


def chunk_sgu(h, w_in, b_in, ln_g, ln_b, w_s, b_s, w_out):
    B, L, _ = h.shape
    z = jax.nn.gelu(h @ w_in + b_in, approximate=False)
    u, v = jnp.split(z, 2, axis=-1)
    v = layer_norm(v, ln_g, ln_b)
    vc = v.reshape(B, L // CHUNK, CHUNK, SG_GROUPS, SG_HALF // SG_GROUPS)
    vs = jnp.einsum('gij,bcjgd->bcigd', w_s, vc) + b_s.T[:, :, None]
    return (u * vs.reshape(B, L, SG_HALF)) @ w_out


def moe_ffn(h, w_router, router_bias, w_gate, w_up, w_down):
    shp = h.shape
    t = h.reshape(-1, shp[-1])
    T = t.shape[0]
    scores = jax.nn.sigmoid((t @ w_router).astype(jnp.float32))
    biased = (scores + router_bias.astype(jnp.float32)).reshape(T, N_EXPERT_GROUPS, -1)
    group_score = jnp.sum(lax.top_k(biased, TOP_K)[0], axis=-1)
    best_group = jnp.argmax(group_score, axis=-1)
    in_group = jnp.arange(N_EXPERT_GROUPS)[None, :] == best_group[:, None]
    masked = jnp.where(in_group[:, :, None], biased, -jnp.inf).reshape(T, N_EXPERTS)
    _, idx = lax.top_k(masked, TOP_K)
    wts = jnp.take_along_axis(scores, idx, axis=-1)
    wts = wts / jnp.sum(wts, axis=-1, keepdims=True)
    gates = jnp.sum(jax.nn.one_hot(idx, N_EXPERTS, dtype=jnp.float32) * wts[..., None], axis=1).astype(t.dtype)
    out = jnp.zeros_like(t)
    for e in range(N_EXPERTS):
        he = jax.nn.silu(t @ w_gate[e]) * (t @ w_up[e])
        out = out + gates[:, e:e + 1] * (he @ w_down[e])
    return out.reshape(shp)


def setup_inputs(seed: int = 0) -> dict:
    key = jax.random.key(seed)
    ks = iter(jax.random.split(key, 64))
    D, E, F = D_MODEL, N_EXPERTS, EXPERT_FFN
    n_a = (DEPTH + N_MIXERS - 1) // N_MIXERS
    n_b = DEPTH // N_MIXERS
    n_v = max(n_a - 1, 0)

    def nrm(shape, scale):
        return jax.random.normal(next(ks), shape, jnp.float32) * scale

    def uni(shape, lo, hi):
        return jax.random.uniform(next(ks), shape, jnp.float32, lo, hi)

    return {
        "x": nrm((BATCH, SEQ, D), 1.0),
        "c": nrm((BATCH, D), 1.0),
        "ctx": nrm((BATCH, CTX_LEN, D), 1.0),
        "c_ctx": nrm((D,), 1.0),
        "w_mod": nrm((DEPTH, D, N_MOD * D), 0.5 * D ** -0.5),
        "b_mod": nrm((DEPTH, N_MOD * D), 0.02),
        "norm1_g": 1.0 + nrm((DEPTH, D), 0.05),
        "norm2_g": 1.0 + nrm((DEPTH, D), 0.05),
        "rw_mu": uni((n_a, 6, D), 0.2, 0.8),
        "rw_wr": nrm((n_a, D, D), D ** -0.5),
        "rw_wk": nrm((n_a, D, D), D ** -0.5),
        "rw_wv": nrm((n_a, D, D), D ** -0.5),
        "rw_wo": nrm((n_a, D, D), D ** -0.5),
        "rw_w0": uni((n_a, N_DIR, D), -6.0, 0.0),
        "rw_w1": nrm((n_a, N_DIR, D, DECAY_LORA), D ** -0.5),
        "rw_w2": nrm((n_a, N_DIR, DECAY_LORA, D), 0.1 * DECAY_LORA ** -0.5),
        "rw_a0": nrm((n_a, N_DIR, D), 0.5),
        "rw_a1": nrm((n_a, N_DIR, D, ICLR_LORA), D ** -0.5),
        "rw_a2": nrm((n_a, N_DIR, ICLR_LORA, D), 0.3 * ICLR_LORA ** -0.5),
        "rw_v0": nrm((n_v, D), 0.5),
        "rw_v1": nrm((n_v, D, VRES_LORA), D ** -0.5),
        "rw_v2": nrm((n_v, VRES_LORA, D), 0.3 * VRES_LORA ** -0.5),
        "rw_g1": nrm((n_a, D, GATE_LORA), D ** -0.5),
        "rw_g2": nrm((n_a, GATE_LORA, D), GATE_LORA ** -0.5),
        "rw_kk": 0.85 + nrm((n_a, D), 0.05),
        "rw_ka": 1.0 + nrm((n_a, D), 0.05),
        "rw_rk": nrm((n_a, N_HEADS, HEAD_SIZE), 0.1),
        "rw_lnx_g": 1.0 + nrm((n_a, D), 0.05),
        "rw_lnx_b": nrm((n_a, D), 0.01),
        "sg_w_in": nrm((n_b, D, SG_FFN), D ** -0.5),
        "sg_b_in": nrm((n_b, SG_FFN), 0.02),
        "sg_ln_g": 1.0 + nrm((n_b, SG_HALF), 0.05),
        "sg_ln_b": nrm((n_b, SG_HALF), 0.01),
        "sg_w_s": nrm((n_b, SG_GROUPS, CHUNK, CHUNK), CHUNK ** -0.5),
        "sg_b_s": 1.0 + nrm((n_b, SG_GROUPS, CHUNK), 0.1),
        "sg_w_out": nrm((n_b, SG_HALF, D), SG_HALF ** -0.5),
        "w_router": nrm((D, E), D ** -0.5),
        "router_bias": nrm((E,), 0.01),
        "ex_w_gate": nrm((DEPTH, E, D, F), D ** -0.5),
        "ex_w_up": nrm((DEPTH, E, D, F), D ** -0.5),
        "ex_w_down": nrm((DEPTH, E, F, D), F ** -0.5),
        "final_g": 1.0 + nrm((D,), 0.05),
    }


def reference(x, c, ctx, c_ctx, w_mod, b_mod, norm1_g, norm2_g, rw_mu, rw_wr, rw_wk, rw_wv, rw_wo,
              rw_w0, rw_w1, rw_w2, rw_a0, rw_a1, rw_a2, rw_v0, rw_v1, rw_v2, rw_g1, rw_g2, rw_kk, rw_ka,
              rw_rk, rw_lnx_g, rw_lnx_b, sg_w_in, sg_b_in, sg_ln_g, sg_ln_b, sg_w_s, sg_b_s, sg_w_out,
              w_router, router_bias, ex_w_gate, ex_w_up, ex_w_down, final_g):
    x_lat, x_ctx = x, ctx
    Lc = ctx.shape[1]
    v_first = None
    for i in range(DEPTH):
        mixer = i % N_MIXERS
        ctx_out = any(j % N_MIXERS == 0 for j in range(i + 1, DEPTH))
        ctx_live = ctx_out or mixer == 0
        sh1, sc1, gt1, sh2, sc2, gt2 = modulation(c, w_mod[i], b_mod[i])
        h_lat = ada_norm(x_lat, norm1_g[i], sh1, sc1)
        h_ctx = None
        if ctx_live:
            csh1, csc1, cgt1, csh2, csc2, cgt2 = modulation(c_ctx, w_mod[i], b_mod[i])
            h_ctx = ada_norm(x_ctx, norm1_g[i], csh1, csc1)

        if mixer == 0:
            a = i // N_MIXERS
            vres = (rw_v0[a - 1], rw_v1[a - 1], rw_v2[a - 1]) if a > 0 else None
            o_ctx, o_lat, v_first = rwkv7_bidir(
                h_ctx, h_lat, v_first, ctx_out, rw_mu[a], rw_wr[a], rw_wk[a], rw_wv[a], rw_wo[a],
                rw_w0[a], rw_w1[a], rw_w2[a], rw_a0[a], rw_a1[a], rw_a2[a], vres,
                rw_g1[a], rw_g2[a], rw_kk[a], rw_ka[a], rw_rk[a], rw_lnx_g[a], rw_lnx_b[a])
        else:
            b = i // N_MIXERS
            h_in = jnp.concatenate([h_ctx, h_lat], axis=1) if ctx_out else h_lat
            o_all = chunk_sgu(h_in, sg_w_in[b], sg_b_in[b], sg_ln_g[b], sg_ln_b[b], sg_w_s[b], sg_b_s[b],
                              sg_w_out[b])
            o_ctx, o_lat = (o_all[:, :Lc], o_all[:, Lc:]) if ctx_out else (None, o_all)

        x_lat = x_lat + gt1 * o_lat
        h_lat = ada_norm(x_lat, norm2_g[i], sh2, sc2)
        if ctx_out:
            x_ctx = x_ctx + cgt1 * o_ctx
            h_ctx = ada_norm(x_ctx, norm2_g[i], csh2, csc2)
            f = moe_ffn(jnp.concatenate([h_ctx, h_lat], axis=1), w_router, router_bias,
                        ex_w_gate[i], ex_w_up[i], ex_w_down[i])
            x_ctx = x_ctx + cgt2 * f[:, :Lc]
            x_lat = x_lat + gt2 * f[:, Lc:]
        else:
            x_lat = x_lat + gt2 * moe_ffn(h_lat, w_router, router_bias, ex_w_gate[i], ex_w_up[i], ex_w_down[i])
    return rms_norm(x_lat, final_g)
```

```python
import functools

import jax
import jax.numpy as jnp
from jax import lax
from jax.experimental import pallas as pl
from jax.experimental.pallas import tpu as pltpu

F32 = jnp.float32
BF16 = jnp.bfloat16

D = 1024
HEAD = 64
PAIRS = D // 128
GRID_W = 64
N_MOD = 6
N_EXPERTS = 16
N_GROUPS = 4
EXPERT_FFN = 512
SG_CHUNK = 128
SG_GROUPS = 16
SG_HALF = 2048
NORM_EPS = 1e-6
LN_EPS = 1e-5
GN_EPS = 64e-5

TM = 256
HALO = GRID_W
WKV_C = 64
MOE_TM = 256
MOD_ROWS = 40
VMEM_LIMIT = 56 * 1024 * 1024


def _cparams(sem):
    return pltpu.CompilerParams(dimension_semantics=sem, vmem_limit_bytes=VMEM_LIMIT)


def _dot(a, b):
    return jnp.dot(a.astype(BF16), b.astype(BF16), preferred_element_type=F32)


def _dot_nt(a, b):
    return lax.dot_general(a.astype(BF16), b.astype(BF16), (((1,), (1,)), ((), ())),
                           preferred_element_type=F32)


def _dot_tn(a, b):
    return lax.dot_general(a.astype(BF16), b.astype(BF16), (((0,), (0,)), ((), ())),
                           preferred_element_type=F32)


def _sigmoid(x):
    return 1.0 / (1.0 + jnp.exp(-x))


def _rmsnorm(x):
    return x * lax.rsqrt(jnp.mean(x * x, axis=-1, keepdims=True) + NORM_EPS)


def _adanorm(x, g, shift, scale):
    return _rmsnorm(x) * g * (1.0 + scale) + shift


def _full(shape):
    zeros = (0,) * len(shape)
    return pl.BlockSpec(shape, lambda *_: zeros)


def _mod_kernel(c_ref, w_ref, b_ref, o_ref):
    c = c_ref[...]
    o_ref[0] = jnp.dot(c * _sigmoid(c), w_ref[0], preferred_element_type=F32) + b_ref[0]


def _mod_call(cvec, w_mod, b_mod):
    depth = w_mod.shape[0]
    return pl.pallas_call(
        _mod_kernel,
        out_shape=jax.ShapeDtypeStruct((depth, MOD_ROWS, N_MOD * D), F32),
        grid=(depth, N_MOD),
        in_specs=[pl.BlockSpec((MOD_ROWS, D), lambda i, n: (0, 0)),
                  pl.BlockSpec((1, D, D), lambda i, n: (i, 0, n)),
                  pl.BlockSpec((1, 1, D), lambda i, n: (i, 0, n))],
        out_specs=pl.BlockSpec((1, MOD_ROWS, D), lambda i, n: (i, 0, n)),
        compiler_params=_cparams(("parallel", "parallel")),
        name="modulation",
    )(cvec, w_mod, b_mod.reshape(depth, 1, N_MOD * D))


def _mod_spec(layer, batch, ctx_tile):
    if ctx_tile:
        return pl.BlockSpec((1, 1, N_MOD, D), lambda b, j: (layer, jnp.where(j == 0, batch, b), 0, 0))
    return pl.BlockSpec((1, 1, N_MOD, D), lambda b, j: (layer, b, 0, 0))


def _rwkv_pre_kernel(*refs, has_f, has_vres, n_tiles):
    it = iter(refs)
    xm, xp, xn = next(it), next(it), next(it)
    if has_f:
        fm, fp, fn, modp = next(it), next(it), next(it), next(it)
    modc, n1g, mu = next(it), next(it), next(it)
    wr, wk, wv = next(it), next(it), next(it)
    w1c, w2b, w0 = next(it), next(it), next(it)
    a1c, a2b, a0 = next(it), next(it), next(it)
    g1, g2 = next(it), next(it)
    if has_vres:
        v0, v1, v2, vf = next(it), next(it), next(it), next(it)
    kkv, kav, rkv, ep = next(it), next(it), next(it), next(it)
    k_o, kap_o, v_o, r_o, a_o, lam_o, g_o, bon_o = (next(it) for _ in range(8))
    if has_f:
        x_o = next(it)

    j = pl.program_id(1)
    x_all = jnp.concatenate([xp[0], xm[0], xn[0]], axis=0)
    if has_f:
        f_all = jnp.concatenate([fp[0], fm[0], fn[0]], axis=0)
        x_all = x_all + modp[0, 0, 5:6, :] * f_all
        x_o[0] = x_all[HALO:HALO + TM]
    m = modc[0, 0]
    h_all = _adanorm(x_all, n1g[...], m[0:1], m[1:2])
    hm = h_all[HALO:HALO + TM]

    q = D // 4
    r = lax.broadcasted_iota(jnp.int32, (TM, q), 0)
    jv = jnp.zeros((TM, q), jnp.int32) + j
    is_ctx = jv == 0
    colp = r % GRID_W
    prev = pltpu.roll(hm, 1, axis=0)
    nxt = pltpu.roll(hm, TM - 1, axis=0)
    up = h_all[0:TM]
    down = h_all[2 * HALO:2 * HALO + TM]
    first, last = r == 0, r == TM - 1
    q0 = jnp.where((is_ctx & first) | (~is_ctx & (colp == 0)), 0.0, prev[:, 0:q])
    q1 = jnp.where(is_ctx, jnp.where(first, 0.0, prev[:, q:2 * q]),
                   jnp.where(colp == GRID_W - 1, 0.0, nxt[:, q:2 * q]))
    q2 = jnp.where(is_ctx, jnp.where(last, 0.0, nxt[:, 2 * q:3 * q]),
                   jnp.where((jv == 1) & (r < HALO), 0.0, up[:, 2 * q:3 * q]))
    q3 = jnp.where(is_ctx, jnp.where(last, 0.0, nxt[:, 3 * q:]),
                   jnp.where((jv == n_tiles - 1) & (r >= TM - HALO), 0.0, down[:, 3 * q:]))
    xx = jnp.concatenate([q0, q1, q2, q3], axis=1) - hm

    def mix(i):
        return (hm + xx * mu[i:i + 1, :]).astype(BF16)

    xr, xw, xk, xv, xa, xg = (mix(i) for i in range(6))
    rr = _dot(xr, wr[...])
    kk = _dot(xk, wk[...])
    vv = _dot(xv, wv[...])
    if has_vres:
        gate = _sigmoid(v0[...] + _dot(_dot(xv, v1[...]), v2[...]))
        vfirst = jnp.concatenate([vf[0, p].astype(F32) for p in range(PAIRS)], axis=1)
        vv = vv + (vfirst - vv) * gate
    g_o[0] = _dot(_sigmoid(_dot(xg, g1[...])), g2[...]).astype(g_o.dtype)

    zw = _dot(jnp.tanh(_dot(xw, w1c[...])), w2b[...])
    za = _dot(_dot(xa, a1c[...]), a2b[...])
    ka = kav[...]
    kd_sum = jnp.zeros((TM, D), F32)
    for d in range(2):
        z = w0[d:d + 1, :] + zw[:, d * D:(d + 1) * D]
        softplus = jnp.maximum(-z, 0.0) + jnp.log(1.0 + jnp.exp(-jnp.abs(z)))
        lam = -jnp.exp(-softplus - 0.5)
        a = _sigmoid(a0[d:d + 1, :] + za[:, d * D:(d + 1) * D])
        kd_sum = kd_sum + kk * (1.0 + (a - 1.0) * ka)
        for p in range(PAIRS):
            sl = slice(128 * p, 128 * (p + 1))
            lam_o[d, 0, p] = lam[:, sl]
            a_o[d, 0, p] = a[:, sl].astype(a_o.dtype)

    kkk = kk * kkv[...]
    rk = rr * rkv[...] * kd_sum
    for p in range(PAIRS):
        sl = slice(128 * p, 128 * (p + 1))
        kp = kkk[:, sl]
        ss = _dot(kp * kp, ep[...])
        kap_o[0, p] = (kp * lax.rsqrt(jnp.maximum(ss, 1e-24))).astype(kap_o.dtype)
        bon_o[0, :, sl] = (_dot(rk[:, sl], ep[...]) * vv[:, sl]).astype(bon_o.dtype)
        k_o[0, p] = kk[:, sl].astype(k_o.dtype)
        v_o[0, p] = vv[:, sl].astype(v_o.dtype)
        r_o[0, p] = rr[:, sl].astype(r_o.dtype)


def _rwkv_pre_call(xs, f_prev, mods, layer, prm, vres, v_first):
    B, T, _ = xs.shape
    nt = T // TM
    nh = T // HALO
    has_f = f_prev is not None
    has_vres = vres is not None
    main = pl.BlockSpec((1, TM, D), lambda b, j: (b, j, 0))
    prev = pl.BlockSpec((1, HALO, D), lambda b, j: (b, jnp.maximum(j * (TM // HALO) - 1, 0), 0))
    nxt = pl.BlockSpec((1, HALO, D), lambda b, j: (b, jnp.minimum((j + 1) * (TM // HALO), nh - 1), 0))
    pm = pl.BlockSpec((1, PAIRS, TM, 128), lambda b, j: (b, 0, j, 0))
    pm2 = pl.BlockSpec((2, 1, PAIRS, TM, 128), lambda b, j: (0, b, 0, j, 0))

    args, specs = [xs, xs, xs], [main, prev, nxt]
    if has_f:
        args += [f_prev, f_prev, f_prev, mods]
        specs += [main, prev, nxt, _mod_spec(layer - 1, B, True)]
    args += [mods, prm["n1g"], prm["mu"], prm["wr"], prm["wk"], prm["wv"], prm["w1c"], prm["w2b"], prm["w0"],
             prm["a1c"], prm["a2b"], prm["a0"], prm["g1"], prm["g2"]]
    specs += [_mod_spec(layer, B, True)] + [_full(a.shape) for a in args[len(specs) + 1:]]
    if has_vres:
        args += [vres["v0"], vres["v1"], vres["v2"], v_first]
        specs += [_full(vres["v0"].shape), _full(vres["v1"].shape), _full(vres["v2"].shape), pm]
    tail = [prm["kk"], prm["ka"], prm["rk"], prm["epair"]]
    args += tail
    specs += [_full(a.shape) for a in tail]

    pm_shape = jax.ShapeDtypeStruct((B, PAIRS, T, 128), BF16)
    out_shape = [pm_shape, pm_shape, pm_shape, pm_shape,
                 jax.ShapeDtypeStruct((2, B, PAIRS, T, 128), BF16),
                 jax.ShapeDtypeStruct((2, B, PAIRS, T, 128), F32),
                 jax.ShapeDtypeStruct((B, T, D), BF16),
                 jax.ShapeDtypeStruct((B, T, D), BF16)]
    out_specs = [pm, pm, pm, pm, pm2, pm2, main, main]
    if has_f:
        out_shape.append(jax.ShapeDtypeStruct((B, T, D), F32))
        out_specs.append(main)
    return pl.pallas_call(
        functools.partial(_rwkv_pre_kernel, has_f=has_f, has_vres=has_vres, n_tiles=nt),
        out_shape=out_shape, grid=(B, nt), in_specs=specs, out_specs=out_specs,
        compiler_params=_cparams(("parallel", "parallel")),
        name="rwkv_pre",
    )(*args)


def _wkv_kernel(k_ref, kap_ref, v_ref, r_ref, a_ref, lam_ref, ka_ref, y_ref, s_ref):
    C = WKV_C
    d = pl.program_id(1)
    j = pl.program_id(2)

    @pl.when(j == 0)
    def _():
        s_ref[...] = jnp.zeros_like(s_ref)

    dv = jnp.zeros((128, 128), jnp.int32) + d
    row = lax.broadcasted_iota(jnp.int32, (128, 128), 0)
    col = lax.broadcasted_iota(jnp.int32, (128, 128), 1)
    same = (row // C) == (col // C)
    tr, tc = row % C, col % C
    before = ((dv == 0) & (tc < tr)) | ((dv != 0) & (tc > tr))
    strict = same & before
    incl = same & (before | (tc == tr))
    eye = row == col
    cum_mask = jnp.where(incl[:C, :C], 1.0, 0.0).astype(BF16)
    lane_lo = lax.broadcasted_iota(jnp.int32, (C, 128), 1) < HEAD

    def stack(x):
        return jnp.concatenate([jnp.where(lane_lo, x, 0.0), jnp.where(lane_lo, 0.0, x)], axis=0)

    def unstack(z):
        return z[:C] + z[C:]

    def body(p, carry):
        lam = lam_ref[0, 0, p]
        kk = k_ref[0, p].astype(F32)
        kap = kap_ref[0, p].astype(F32)
        v = v_ref[0, p].astype(F32)
        r = r_ref[0, p].astype(F32)
        a = a_ref[0, 0, p].astype(F32)
        ka = ka_ref[p]

        lam_hi = lam.astype(BF16)
        lam_lo = (lam - lam_hi.astype(F32)).astype(BF16)
        cs = jnp.dot(cum_mask, jnp.concatenate([lam_hi, lam_lo], axis=1), preferred_element_type=F32)
        lc = cs[:, :128] + cs[:, 128:]
        ltot = jnp.sum(lam, axis=0, keepdims=True)

        e_neg = jnp.exp(-lc)
        kd = kk * (1.0 + (a - 1.0) * ka)
        bet = kap * a
        kt = kd * e_neg
        bt = bet * e_neg
        qk = kap * jnp.exp(lc - lam)
        rt = r * jnp.exp(lc)
        e_tail = jnp.exp(ltot - lc)
        kh = kd * e_tail
        bh = bet * e_tail

        qk_z = stack(qk)
        rt_z = stack(rt)
        v_z = stack(v)
        sc = _dot_nt(jnp.concatenate([qk_z, rt_z], axis=0),
                     jnp.concatenate([kt, kt, bt, bt], axis=0))
        m_k = jnp.where(strict, sc[:128, :128], 0.0)
        m_b = jnp.where(strict, sc[:128, 128:], 0.0)
        a_k = jnp.where(incl, sc[128:, :128], 0.0)
        a_b = jnp.where(incl, sc[128:, 128:], 0.0)

        pw = -m_b
        t_inv = jnp.where(eye, 1.0, 0.0) + pw
        for _ in range(C.bit_length() - 2):
            pw = _dot(pw, pw)
            t_inv = t_inv + _dot(t_inv, pw)

        mkv = _dot(m_k, v_z)
        tw = _dot(t_inv, jnp.concatenate([qk_z, mkv], axis=1))
        ab = _dot(a_b, tw)
        akv = _dot(a_k, v_z)
        rhat = unstack(rt_z - ab[:, :128])
        yloc = unstack(akv - ab[:, 128:])
        gh = _dot_tn(stack(bh), tw)
        khv = _dot_tn(stack(kh), v_z)
        g = jnp.where(eye, jnp.exp(ltot), 0.0) - gh[:, :128]
        hm = khv - gh[:, 128:]

        s0 = s_ref[p]
        s_hi = s0.astype(BF16)
        s_lo = (s0 - s_hi.astype(F32)).astype(BF16)
        gs = _dot(jnp.concatenate([g, rhat], axis=0), jnp.concatenate([s_hi, s_lo], axis=1))
        gs = gs[:, :128] + gs[:, 128:]
        s_ref[p] = gs[:128] + hm
        y_ref[0, 0, p] = (gs[128:] + yloc).astype(y_ref.dtype)
        return carry

    lax.fori_loop(0, PAIRS, body, 0)


def _wkv_call(k, kap, v, r, a, lam, ka, ctx_len, out_dtype=BF16):
    B, _, T, _ = k.shape
    C = WKV_C
    ns = T // C
    nc = ctx_len // C

    def chunk(d, j):
        rev = jnp.where(j < nc, nc - 1 - j, ns - 1 + nc - j)
        return jnp.where(d == 0, j, rev)

    tok = pl.BlockSpec((1, PAIRS, C, 128), lambda b, d, j: (b, 0, chunk(d, j), 0))
    per_dir = pl.BlockSpec((1, 1, PAIRS, C, 128), lambda b, d, j: (d, b, 0, chunk(d, j), 0))
    return pl.pallas_call(
        _wkv_kernel,
        out_shape=jax.ShapeDtypeStruct((2, B, PAIRS, T, 128), out_dtype),
        grid=(B, 2, ns),
        in_specs=[tok, tok, tok, tok, per_dir, per_dir, _full((PAIRS, 1, 128))],
        out_specs=per_dir,
        scratch_shapes=[pltpu.VMEM((PAIRS, 128, 128), F32)],
        compiler_params=_cparams(("parallel", "parallel", "arbitrary")),
        name="wkv_chunked",
    )(k, kap, v, r, a, lam, ka)


def _mixer_tail(x, mix_out, m, n2g, wrt, x_o, h_o, lg_o):
    x_new = x + m[2:3] * mix_out
    h2 = _adanorm(x_new, n2g[...], m[3:4], m[4:5])
    x_o[0] = x_new
    h_o[0] = h2.astype(h_o.dtype)
    lg_o[0] = lax.dot_general(wrt[...], h2, (((1,), (1,)), ((), ())), preferred_element_type=F32)


def _tail_out(B, nt):
    t_out = nt * TM
    shapes = [jax.ShapeDtypeStruct((B, t_out, D), F32), jax.ShapeDtypeStruct((B, t_out, D), BF16),
              jax.ShapeDtypeStruct((B, N_EXPERTS, t_out), F32)]
    tile = pl.BlockSpec((1, TM, D), lambda b, j: (b, j, 0))
    specs = [tile, tile, pl.BlockSpec((1, N_EXPERTS, TM), lambda b, j: (b, 0, j))]
    return shapes, specs


def _rwkv_post_kernel(y0, y1, bon, g, x, modc, lng, lnb, wo, n2g, wrt, ep, x_o, h_o, lg_o):
    parts = []
    for p in range(PAIRS):
        y = y0[0, 0, p].astype(F32) + y1[0, 0, p].astype(F32)
        dlt = y - _dot(y, ep[...]) * (1.0 / HEAD)
        var = _dot(dlt * dlt, ep[...]) * (1.0 / HEAD)
        parts.append(dlt * lax.rsqrt(var + GN_EPS))
    yn = jnp.concatenate(parts, axis=1)
    z = (yn * lng[...] + lnb[...] + bon[0].astype(F32)) * g[0].astype(F32)
    _mixer_tail(x[0], _dot(z, wo[...]), modc[0, 0], n2g, wrt, x_o, h_o, lg_o)


def _rwkv_post_call(y, bonus, g, x, mods, layer, prm, n2g, wrt, ctx_out):
    B, T, _ = x.shape
    nt = T // TM
    off = 0 if ctx_out else 1
    tile = pl.BlockSpec((1, TM, D), lambda b, j: (b, j + off, 0))
    ydir = [pl.BlockSpec((1, 1, PAIRS, TM, 128), functools.partial(lambda b, j, d: (d, b, 0, j + off, 0), d=d))
            for d in range(2)]
    consts = [prm["lng"], prm["lnb"], prm["wo"], n2g, wrt, prm["epair"]]
    shapes, ospecs = _tail_out(B, nt - off)
    return pl.pallas_call(
        _rwkv_post_kernel, out_shape=shapes, grid=(B, nt - off),
        in_specs=ydir + [tile, tile, tile, _mod_spec(layer, B, ctx_out)] + [_full(a.shape) for a in consts],
        out_specs=ospecs,
        compiler_params=_cparams(("parallel", "parallel")),
        name="rwkv_post",
    )(y, y, bonus, g, x, mods, *consts)


def _sgu_kernel(x, f, modp, modc, n1g, win, bin_, lng, lnb, ws, bs, wout, n2g, wrt, x_o, h_o, lg_o):
    x1 = x[0] + modp[0, 0, 5:6, :] * f[0]
    m = modc[0, 0]
    h = _adanorm(x1, n1g[...], m[0:1], m[1:2])
    z = _dot(h, win[...]) + bin_[...]
    z = 0.5 * z * (1.0 + lax.erf(z * 0.7071067811865476))
    u, v = z[:, :SG_HALF], z[:, SG_HALF:]
    mu = jnp.mean(v, axis=-1, keepdims=True)
    dv = v - mu
    var = jnp.mean(dv * dv, axis=-1, keepdims=True)
    vn = (dv * lax.rsqrt(var + LN_EPS) * lng[...] + lnb[...]).astype(BF16)
    gw = SG_HALF // SG_GROUPS
    rows = []
    for c in range(TM // SG_CHUNK):
        rs = slice(c * SG_CHUNK, (c + 1) * SG_CHUNK)
        cols = []
        for gi in range(SG_GROUPS):
            cs = slice(gi * gw, (gi + 1) * gw)
            vs = jnp.dot(ws[gi], vn[rs, cs], preferred_element_type=F32) + bs[gi]
            cols.append((u[rs, cs] * vs).astype(BF16))
        rows.append(jnp.concatenate(cols, axis=1))
    gated = jnp.concatenate(rows, axis=0)
    _mixer_tail(x1, _dot(gated, wout[...]), m, n2g, wrt, x_o, h_o, lg_o)


def _sgu_call(x, f, mods, layer, prm, n2g, wrt, has_ctx):
    B, T, _ = x.shape
    nt = T // TM
    tile = pl.BlockSpec((1, TM, D), lambda b, j: (b, j, 0))
    consts = [prm["n1g"], prm["win"], prm["bin"], prm["lng"], prm["lnb"], prm["ws"], prm["bs"], prm["wout"],
              n2g, wrt]
    shapes, ospecs = _tail_out(B, nt)
    return pl.pallas_call(
        _sgu_kernel, out_shape=shapes, grid=(B, nt),
        in_specs=[tile, tile, _mod_spec(layer - 1, B, has_ctx), _mod_spec(layer, B, has_ctx)]
        + [_full(a.shape) for a in consts],
        out_specs=ospecs,
        compiler_params=_cparams(("parallel", "parallel")),
        name="chunk_sgu",
    )(x, f, mods, mods, *consts)


def _route_kernel(lg_ref, rb_ref, slot_ref, wt_ref, meta_ref, *, W):
    E, G = N_EXPERTS, N_GROUPS
    per = E // G
    s = _sigmoid(lg_ref[0])
    biased = s + rb_ref[:, 0:1]
    eidx = lax.broadcasted_iota(jnp.int32, (E, W), 0)

    best = None
    for gi in range(G):
        a, b, c, d = (biased[per * gi + i:per * gi + i + 1, :] for i in range(per))
        hi1, lo1, hi2, lo2 = jnp.maximum(a, b), jnp.minimum(a, b), jnp.maximum(c, d), jnp.minimum(c, d)
        top2 = jnp.maximum(hi1, hi2) + jnp.maximum(jnp.minimum(hi1, hi2), jnp.maximum(lo1, lo2))
        if best is None:
            best, bg = top2, jnp.zeros((1, W), jnp.int32)
        else:
            upd = top2 > best
            bg = jnp.where(upd, gi, bg)
            best = jnp.where(upd, top2, best)
    neg = -jnp.inf
    m1 = jnp.where((eidx // per) == bg, biased, neg)
    i1 = jnp.min(jnp.where(m1 == jnp.max(m1, axis=0, keepdims=True), eidx, E), axis=0, keepdims=True)
    sel1 = eidx == i1
    m2 = jnp.where(sel1, neg, m1)
    i2 = jnp.min(jnp.where(m2 == jnp.max(m2, axis=0, keepdims=True), eidx, E), axis=0, keepdims=True)
    sel2 = eidx == i2
    w1 = jnp.sum(jnp.where(sel1, s, 0.0), axis=0, keepdims=True)
    w2 = jnp.sum(jnp.where(sel2, s, 0.0), axis=0, keepdims=True)
    wt_ref[0] = jnp.concatenate([w1 / (w1 + w2), w2 / (w1 + w2)], axis=0)

    onehot = jnp.where(sel1 | sel2, 1.0, 0.0)
    blk = 256
    upper = jnp.where(lax.broadcasted_iota(jnp.int32, (blk, blk), 0) < lax.broadcasted_iota(jnp.int32, (blk, blk), 1),
                      1.0, 0.0).astype(BF16)
    carry = jnp.zeros((E, 1), F32)
    ranks = []
    for i in range(W // blk):
        ob = onehot[:, i * blk:(i + 1) * blk]
        ranks.append(jnp.dot(ob.astype(BF16), upper, preferred_element_type=F32) + carry)
        carry = carry + jnp.sum(ob, axis=1, keepdims=True)
    rank = jnp.concatenate(ranks, axis=1)
    padded = jnp.ceil(carry * (1.0 / MOE_TM)) * MOE_TM
    offs, cum = [], jnp.zeros((1, 1), F32)
    lane_tile = lax.broadcasted_iota(jnp.int32, (1, 128), 1).astype(F32) * MOE_TM
    texp = jnp.zeros((1, 128), jnp.int32)
    e_last = jnp.zeros((1, 1), jnp.int32)
    for e in range(E):
        offs.append(cum)
        pe = padded[e:e + 1, :]
        cum = cum + pe
        texp = texp + jnp.where(cum <= lane_tile, 1, 0)
        e_last = jnp.where(pe > 0.0, e, e_last)
    pos = jnp.concatenate(offs, axis=0) + rank
    slot1 = jnp.sum(jnp.where(sel1, pos, 0.0), axis=0, keepdims=True)
    slot2 = jnp.sum(jnp.where(sel2, pos, 0.0), axis=0, keepdims=True)
    slot_ref[0] = jnp.concatenate([slot1, slot2], axis=0).astype(jnp.int32)
    n_tiles = (cum * (1.0 / MOE_TM)).astype(jnp.int32)
    meta_ref[0] = jnp.concatenate([jnp.minimum(texp, e_last), jnp.zeros((1, 128), jnp.int32) + n_tiles,
                                   jnp.zeros((6, 128), jnp.int32)], axis=0)


def _route_call(logits, rbias):
    nw, _, W = logits.shape
    return pl.pallas_call(
        functools.partial(_route_kernel, W=W),
        out_shape=[jax.ShapeDtypeStruct((nw, 2, W), jnp.int32), jax.ShapeDtypeStruct((nw, 2, W), F32),
                   jax.ShapeDtypeStruct((nw, 8, 128), jnp.int32)],
        grid=(nw,),
        in_specs=[pl.BlockSpec((1, N_EXPERTS, W), lambda w: (w, 0, 0)), _full(rbias.shape)],
        out_specs=[pl.BlockSpec((1, 2, W), lambda w: (w, 0, 0)), pl.BlockSpec((1, 2, W), lambda w: (w, 0, 0)),
                   pl.BlockSpec((1, 8, 128), lambda w: (w, 0, 0))],
        compiler_params=_cparams(("parallel",)),
        name="moe_route",
    )(logits, rbias)


def _moe_kernel(texp_ref, nt_ref, h_ref, slot_ref, col_ref, wgu_ref, wd_ref, o_ref, *, W):
    w = pl.program_id(0)
    t = pl.program_id(1)

    @pl.when(t == 0)
    def _():
        o_ref[...] = jnp.zeros_like(o_ref)

    @pl.when(t < nt_ref[w])
    def _():
        base = t * MOE_TM
        rid = lax.broadcasted_iota(jnp.int32, (MOE_TM, W), 0) + base
        hit = (slot_ref[0, 0:1, :] == rid) | (slot_ref[0, 1:2, :] == rid)
        disp = jnp.where(hit, 1.0, 0.0).astype(BF16)
        xt = jnp.dot(disp, h_ref[0], preferred_element_type=F32).astype(BF16)
        gu = jnp.dot(xt, wgu_ref[0], preferred_element_type=F32)
        gate, up = gu[:, :EXPERT_FFN], gu[:, EXPERT_FFN:]
        y = _dot(gate * _sigmoid(gate) * up, wd_ref[0]).astype(BF16)
        cid = (lax.broadcasted_iota(jnp.int32, (W, MOE_TM), 1) + base).astype(F32)
        col = col_ref[0]
        comb = (jnp.where(col[:, 0:1] == cid, col[:, 2:3], 0.0)
                + jnp.where(col[:, 1:2] == cid, col[:, 3:4], 0.0)).astype(BF16)
        o_ref[0] += jnp.dot(comb, y, preferred_element_type=F32)


def _moe_call(h2, slots, colinfo, texp, ntiles, wgu, wd):
    nw, W, _ = h2.shape
    max_tiles = texp.shape[0] // nw
    win = lambda w, t, te, nt: (w, 0, 0)
    expert = lambda w, t, te, nt: (te[w * max_tiles + t], 0, 0)
    grid_spec = pltpu.PrefetchScalarGridSpec(
        num_scalar_prefetch=2, grid=(nw, max_tiles),
        in_specs=[pl.BlockSpec((1, W, D), win), pl.BlockSpec((1, 2, W), win), pl.BlockSpec((1, W, 4), win),
                  pl.BlockSpec((1, D, 2 * EXPERT_FFN), expert), pl.BlockSpec((1, EXPERT_FFN, D), expert)],
        out_specs=pl.BlockSpec((1, W, D), win))
    return pl.pallas_call(
        functools.partial(_moe_kernel, W=W),
        out_shape=jax.ShapeDtypeStruct((nw, W, D), F32),
        grid_spec=grid_spec,
        compiler_params=_cparams(("parallel", "arbitrary")),
        name="moe_experts",
    )(texp, ntiles, h2, slots, colinfo, wgu, wd)


def _moe_layer(h2, logits, rbias, wgu, wd):
    nw, W, _ = h2.shape
    slots, wts, meta = _route_call(logits, rbias)
    max_tiles = 2 * W // MOE_TM + N_EXPERTS
    texp = meta[:, 0, :max_tiles].reshape(-1)
    ntiles = meta[:, 1, 0]
    colinfo = jnp.concatenate([slots.astype(F32), wts], axis=1).transpose(0, 2, 1)
    return _moe_call(h2, slots, colinfo, texp, ntiles, wgu, wd)


def _final_kernel(x, f, modp, g, o):
    o[0] = _rmsnorm(x[0] + modp[0, 0, 5:6, :] * f[0]) * g[...]


def _final_call(x, f, mods, layer, final_g):
    B, T, _ = x.shape
    tile = pl.BlockSpec((1, TM, D), lambda b, j: (b, j, 0))
    return pl.pallas_call(
        _final_kernel, out_shape=jax.ShapeDtypeStruct((B, T, D), F32), grid=(B, T // TM),
        in_specs=[tile, tile, _mod_spec(layer, B, False), _full(final_g.shape)], out_specs=tile,
        compiler_params=_cparams(("parallel", "parallel")),
        name="final_norm",
    )(x, f, mods, final_g)


def _rwkv_params(a, norm1_g, layer, rw_mu, rw_wr, rw_wk, rw_wv, rw_wo, rw_w0, rw_w1, rw_w2, rw_a0, rw_a1, rw_a2,
                 rw_g1, rw_g2, rw_kk, rw_ka, rw_rk, rw_lnx_g, rw_lnx_b, epair):
    lora = rw_w1.shape[-1]

    def block_diag(w2):
        z = jnp.zeros((lora, D), w2.dtype)
        return jnp.concatenate([jnp.concatenate([w2[0], z], axis=1), jnp.concatenate([z, w2[1]], axis=1)], axis=0)

    return dict(
        n1g=norm1_g[layer].reshape(1, D), mu=rw_mu[a],
        wr=rw_wr[a].astype(BF16), wk=rw_wk[a].astype(BF16), wv=rw_wv[a].astype(BF16), wo=rw_wo[a].astype(BF16),
        w1c=jnp.concatenate([rw_w1[a, 0], rw_w1[a, 1]], axis=1).astype(BF16), w2b=block_diag(rw_w2[a]).astype(BF16),
        w0=rw_w0[a],
        a1c=jnp.concatenate([rw_a1[a, 0], rw_a1[a, 1]], axis=1).astype(BF16), a2b=block_diag(rw_a2[a]).astype(BF16),
        a0=rw_a0[a],
        g1=rw_g1[a].astype(BF16), g2=rw_g2[a].astype(BF16),
        kk=rw_kk[a].reshape(1, D), ka=rw_ka[a].reshape(1, D), rk=rw_rk[a].reshape(1, D),
        lng=rw_lnx_g[a].reshape(1, D), lnb=rw_lnx_b[a].reshape(1, D), epair=epair)


def kernel(x, c, ctx, c_ctx, w_mod, b_mod, norm1_g, norm2_g, rw_mu, rw_wr, rw_wk, rw_wv, rw_wo, rw_w0, rw_w1, rw_w2, rw_a0, rw_a1, rw_a2, rw_v0, rw_v1, rw_v2, rw_g1, rw_g2, rw_kk, rw_ka, rw_rk, rw_lnx_g, rw_lnx_b, sg_w_in, sg_b_in, sg_ln_g, sg_ln_b, sg_w_s, sg_b_s, sg_w_out, w_router, router_bias, ex_w_gate, ex_w_up, ex_w_down, final_g):
    B, L, _ = x.shape
    Lc = ctx.shape[1]
    assert Lc == TM and L % TM == 0 and B + 1 <= MOD_ROWS and w_mod.shape[0] == 4

    cvec = jnp.concatenate([c, c_ctx[None, :], jnp.zeros((MOD_ROWS - B - 1, D), F32)], axis=0)
    mods = _mod_call(cvec, w_mod, b_mod).reshape(w_mod.shape[0], MOD_ROWS, N_MOD, D)
    xs = jnp.concatenate([ctx, x], axis=1)

    lane = jnp.arange(128) // HEAD
    epair = (lane[:, None] == lane[None, :]).astype(BF16)
    wrt = w_router.T
    rbias = jnp.broadcast_to(router_bias[:, None], (N_EXPERTS, 128))
    n2g = [norm2_g[i].reshape(1, D) for i in range(4)]
    wgu = [jnp.concatenate([ex_w_gate[i], ex_w_up[i]], axis=-1).astype(BF16) for i in range(4)]
    wd = [ex_w_down[i].astype(BF16) for i in range(4)]
    rw_common = (rw_mu, rw_wr, rw_wk, rw_wv, rw_wo, rw_w0, rw_w1, rw_w2, rw_a0, rw_a1, rw_a2, rw_g1, rw_g2,
                 rw_kk, rw_ka, rw_rk, rw_lnx_g, rw_lnx_b, epair)

    def sgu_params(b, layer):
        return dict(n1g=norm1_g[layer].reshape(1, D), win=sg_w_in[b].astype(BF16), bin=sg_b_in[b].reshape(1, -1),
                    lng=sg_ln_g[b].reshape(1, -1), lnb=sg_ln_b[b].reshape(1, -1), ws=sg_w_s[b].astype(BF16),
                    bs=jnp.broadcast_to(sg_b_s[b][:, :, None], (SG_GROUPS, SG_CHUNK, SG_HALF // SG_GROUPS)),
                    wout=sg_w_out[b].astype(BF16))

    p0 = _rwkv_params(0, norm1_g, 0, *rw_common)
    k0, kap0, v0, r0, a0, lam0, g0, bon0 = _rwkv_pre_call(xs, None, mods, 0, p0, None, None)
    y0 = _wkv_call(k0, kap0, v0, r0, a0, lam0, p0["ka"].reshape(PAIRS, 1, 128), Lc)
    x1, h1, lg1 = _rwkv_post_call(y0, bon0, g0, xs, mods, 0, p0, n2g[0], wrt, True)
    f0 = _moe_layer(h1, lg1, rbias, wgu[0], wd[0])

    x2, h2, lg2 = _sgu_call(x1, f0, mods, 1, sgu_params(0, 1), n2g[1], wrt, True)
    f1 = _moe_layer(h2, lg2, rbias, wgu[1], wd[1])

    p2 = _rwkv_params(1, norm1_g, 2, *rw_common)
    vres = dict(v0=rw_v0[0].reshape(1, D),
                v1=jnp.pad(rw_v1[0], ((0, 0), (0, 128 - rw_v1.shape[-1]))).astype(BF16),
                v2=jnp.pad(rw_v2[0], ((0, 128 - rw_v2.shape[-2]), (0, 0))).astype(BF16))
    k2, kap2, v2, r2, a2, lam2, g2, bon2, x2r = _rwkv_pre_call(x2, f1, mods, 2, p2, vres, v0)
    y2 = _wkv_call(k2, kap2, v2, r2, a2, lam2, p2["ka"].reshape(PAIRS, 1, 128), Lc)
    x3, h3, lg3 = _rwkv_post_call(y2, bon2, g2, x2r, mods, 2, p2, n2g[2], wrt, False)
    f2 = _moe_layer(h3, lg3, rbias, wgu[2], wd[2])

    x4, h4, lg4 = _sgu_call(x3, f2, mods, 3, sgu_params(1, 3), n2g[3], wrt, False)
    f3 = _moe_layer(h4, lg4, rbias, wgu[3], wd[3])
    return _final_call(x4, f3, mods, 3, final_g.reshape(1, D))
```

```python
import functools

import jax
import jax.numpy as jnp
from jax import lax
from jax.experimental import pallas as pl
from jax.experimental.pallas import tpu as pltpu

F32 = jnp.float32
BF16 = jnp.bfloat16

D = 1024
HEAD = 64
PAIRS = D // 128
GRID_W = 64
N_MOD = 6
N_EXPERTS = 16
N_GROUPS = 4
EXPERT_FFN = 512
SG_CHUNK = 128
SG_GROUPS = 16
SG_HALF = 2048
NORM_EPS = 1e-6
LN_EPS = 1e-5
GN_EPS = 64e-5

TM = 256
HALO = GRID_W
WKV_C = 64
WKV_GROUP = 8
MOE_TM = 256
MOD_ROWS = 40
VMEM_LIMIT = 56 * 1024 * 1024


def _cparams(sem):
    return pltpu.CompilerParams(dimension_semantics=sem, vmem_limit_bytes=VMEM_LIMIT)


def _dot(a, b):
    return jnp.dot(a.astype(BF16), b.astype(BF16), preferred_element_type=F32)


def _dot_nt(a, b):
    return lax.dot_general(a.astype(BF16), b.astype(BF16), (((1,), (1,)), ((), ())),
                           preferred_element_type=F32)


def _dot_tn(a, b):
    return lax.dot_general(a.astype(BF16), b.astype(BF16), (((0,), (0,)), ((), ())),
                           preferred_element_type=F32)


def _sigmoid(x):
    return 1.0 / (1.0 + jnp.exp(-x))


def _rmsnorm(x):
    return x * lax.rsqrt(jnp.mean(x * x, axis=-1, keepdims=True) + NORM_EPS)


def _adanorm(x, g, shift, scale):
    return _rmsnorm(x) * g * (1.0 + scale) + shift


def _full(shape):
    zeros = (0,) * len(shape)
    return pl.BlockSpec(shape, lambda *_: zeros)


def _mod_kernel(c_ref, w_ref, b_ref, o_ref):
    c = c_ref[...]
    o_ref[0] = jnp.dot(c * _sigmoid(c), w_ref[0], preferred_element_type=F32) + b_ref[0]


def _mod_call(cvec, w_mod, b_mod):
    depth = w_mod.shape[0]
    return pl.pallas_call(
        _mod_kernel,
        out_shape=jax.ShapeDtypeStruct((depth, MOD_ROWS, N_MOD * D), F32),
        grid=(depth, N_MOD),
        in_specs=[pl.BlockSpec((MOD_ROWS, D), lambda i, n: (0, 0)),
                  pl.BlockSpec((1, D, D), lambda i, n: (i, 0, n)),
                  pl.BlockSpec((1, 1, D), lambda i, n: (i, 0, n))],
        out_specs=pl.BlockSpec((1, MOD_ROWS, D), lambda i, n: (i, 0, n)),
        compiler_params=_cparams(("parallel", "parallel")),
        name="modulation",
    )(cvec, w_mod, b_mod.reshape(depth, 1, N_MOD * D))


def _mod_spec(layer, batch, ctx_tile):
    if ctx_tile:
        return pl.BlockSpec((1, 1, N_MOD, D), lambda b, j: (layer, jnp.where(j == 0, batch, b), 0, 0))
    return pl.BlockSpec((1, 1, N_MOD, D), lambda b, j: (layer, b, 0, 0))


def _rwkv_pre_kernel(*refs, has_f, has_vres, n_tiles):
    it = iter(refs)
    xm, xp, xn = next(it), next(it), next(it)
    if has_f:
        fm, fp, fn, modp = next(it), next(it), next(it), next(it)
    modc, n1g, mu = next(it), next(it), next(it)
    wr, wk, wv = next(it), next(it), next(it)
    w1c, w2b, w0 = next(it), next(it), next(it)
    a1c, a2b, a0 = next(it), next(it), next(it)
    g1, g2 = next(it), next(it)
    if has_vres:
        v0, v1, v2, vf = next(it), next(it), next(it), next(it)
    kkv, kav, rkv, ep = next(it), next(it), next(it), next(it)
    k_o, kap_o, v_o, r_o, a_o, lam_o, g_o, bon_o = (next(it) for _ in range(8))
    if has_f:
        x_o = next(it)

    j = pl.program_id(1)
    x_all = jnp.concatenate([xp[0], xm[0], xn[0]], axis=0)
    if has_f:
        f_all = jnp.concatenate([fp[0], fm[0], fn[0]], axis=0)
        x_all = x_all + modp[0, 0, 5:6, :] * f_all
        x_o[0] = x_all[HALO:HALO + TM]
    m = modc[0, 0]
    h_all = _adanorm(x_all, n1g[...], m[0:1], m[1:2])
    hm = h_all[HALO:HALO + TM]

    q = D // 4
    r = lax.broadcasted_iota(jnp.int32, (TM, q), 0)
    jv = jnp.zeros((TM, q), jnp.int32) + j
    is_ctx = jv == 0
    colp = r % GRID_W
    prev = pltpu.roll(hm, 1, axis=0)
    nxt = pltpu.roll(hm, TM - 1, axis=0)
    up = h_all[0:TM]
    down = h_all[2 * HALO:2 * HALO + TM]
    first, last = r == 0, r == TM - 1
    q0 = jnp.where((is_ctx & first) | (~is_ctx & (colp == 0)), 0.0, prev[:, 0:q])
    q1 = jnp.where(is_ctx, jnp.where(first, 0.0, prev[:, q:2 * q]),
                   jnp.where(colp == GRID_W - 1, 0.0, nxt[:, q:2 * q]))
    q2 = jnp.where(is_ctx, jnp.where(last, 0.0, nxt[:, 2 * q:3 * q]),
                   jnp.where((jv == 1) & (r < HALO), 0.0, up[:, 2 * q:3 * q]))
    q3 = jnp.where(is_ctx, jnp.where(last, 0.0, nxt[:, 3 * q:]),
                   jnp.where((jv == n_tiles - 1) & (r >= TM - HALO), 0.0, down[:, 3 * q:]))
    xx = jnp.concatenate([q0, q1, q2, q3], axis=1) - hm

    def mix(i):
        return (hm + xx * mu[i:i + 1, :]).astype(BF16)

    xr, xw, xk, xv, xa, xg = (mix(i) for i in range(6))
    rr = _dot(xr, wr[...])
    kk = _dot(xk, wk[...])
    vv = _dot(xv, wv[...])
    if has_vres:
        gate = _sigmoid(v0[...] + _dot(_dot(xv, v1[...]), v2[...]))
        vfirst = jnp.concatenate([vf[0, p].astype(F32) for p in range(PAIRS)], axis=1)
        vv = vv + (vfirst - vv) * gate
    g_o[0] = _dot(_sigmoid(_dot(xg, g1[...])), g2[...]).astype(g_o.dtype)

    zw = _dot(jnp.tanh(_dot(xw, w1c[...])), w2b[...])
    za = _dot(_dot(xa, a1c[...]), a2b[...])
    ka = kav[...]
    kd_sum = jnp.zeros((TM, D), F32)
    for d in range(2):
        z = w0[d:d + 1, :] + zw[:, d * D:(d + 1) * D]
        softplus = jnp.maximum(-z, 0.0) + jnp.log(1.0 + jnp.exp(-jnp.abs(z)))
        lam = -jnp.exp(-softplus - 0.5)
        a = _sigmoid(a0[d:d + 1, :] + za[:, d * D:(d + 1) * D])
        kd_sum = kd_sum + kk * (1.0 + (a - 1.0) * ka)
        for p in range(PAIRS):
            sl = slice(128 * p, 128 * (p + 1))
            lam_o[d, 0, p] = lam[:, sl]
            a_o[d, 0, p] = a[:, sl].astype(a_o.dtype)

    kkk = kk * kkv[...]
    rk = rr * rkv[...] * kd_sum
    for p in range(PAIRS):
        sl = slice(128 * p, 128 * (p + 1))
        kp = kkk[:, sl]
        ss = _dot(kp * kp, ep[...])
        kap_o[0, p] = (kp * lax.rsqrt(jnp.maximum(ss, 1e-24))).astype(kap_o.dtype)
        bon_o[0, :, sl] = (_dot(rk[:, sl], ep[...]) * vv[:, sl]).astype(bon_o.dtype)
        k_o[0, p] = kk[:, sl].astype(k_o.dtype)
        v_o[0, p] = vv[:, sl].astype(v_o.dtype)
        r_o[0, p] = rr[:, sl].astype(r_o.dtype)


def _rwkv_pre_call(xs, f_prev, mods, layer, prm, vres, v_first):
    B, T, _ = xs.shape
    nt = T // TM
    nh = T // HALO
    has_f = f_prev is not None
    has_vres = vres is not None
    main = pl.BlockSpec((1, TM, D), lambda b, j: (b, j, 0))
    prev = pl.BlockSpec((1, HALO, D), lambda b, j: (b, jnp.maximum(j * (TM // HALO) - 1, 0), 0))
    nxt = pl.BlockSpec((1, HALO, D), lambda b, j: (b, jnp.minimum((j + 1) * (TM // HALO), nh - 1), 0))
    pm = pl.BlockSpec((1, PAIRS, TM, 128), lambda b, j: (b, 0, j, 0))
    pm2 = pl.BlockSpec((2, 1, PAIRS, TM, 128), lambda b, j: (0, b, 0, j, 0))

    args, specs = [xs, xs, xs], [main, prev, nxt]
    if has_f:
        args += [f_prev, f_prev, f_prev, mods]
        specs += [main, prev, nxt, _mod_spec(layer - 1, B, True)]
    args += [mods, prm["n1g"], prm["mu"], prm["wr"], prm["wk"], prm["wv"], prm["w1c"], prm["w2b"], prm["w0"],
             prm["a1c"], prm["a2b"], prm["a0"], prm["g1"], prm["g2"]]
    specs += [_mod_spec(layer, B, True)] + [_full(a.shape) for a in args[len(specs) + 1:]]
    if has_vres:
        args += [vres["v0"], vres["v1"], vres["v2"], v_first]
        specs += [_full(vres["v0"].shape), _full(vres["v1"].shape), _full(vres["v2"].shape), pm]
    tail = [prm["kk"], prm["ka"], prm["rk"], prm["epair"]]
    args += tail
    specs += [_full(a.shape) for a in tail]

    pm_shape = jax.ShapeDtypeStruct((B, PAIRS, T, 128), BF16)
    out_shape = [pm_shape, pm_shape, pm_shape, pm_shape,
                 jax.ShapeDtypeStruct((2, B, PAIRS, T, 128), BF16),
                 jax.ShapeDtypeStruct((2, B, PAIRS, T, 128), F32),
                 jax.ShapeDtypeStruct((B, T, D), BF16),
                 jax.ShapeDtypeStruct((B, T, D), BF16)]
    out_specs = [pm, pm, pm, pm, pm2, pm2, main, main]
    if has_f:
        out_shape.append(jax.ShapeDtypeStruct((B, T, D), F32))
        out_specs.append(main)
    return pl.pallas_call(
        functools.partial(_rwkv_pre_kernel, has_f=has_f, has_vres=has_vres, n_tiles=nt),
        out_shape=out_shape, grid=(B, nt), in_specs=specs, out_specs=out_specs,
        compiler_params=_cparams(("parallel", "parallel")),
        name="rwkv_pre",
    )(*args)


def _wkv_kernel(k_ref, kap_ref, v_ref, r_ref, a_ref, lam_ref, ka_ref, y_ref, s_ref):
    C = WKV_C
    d = pl.program_id(1)
    j = pl.program_id(2)

    @pl.when(j == 0)
    def _():
        s_ref[...] = jnp.zeros_like(s_ref)

    dv = jnp.zeros((128, 128), jnp.int32) + d
    row = lax.broadcasted_iota(jnp.int32, (128, 128), 0)
    col = lax.broadcasted_iota(jnp.int32, (128, 128), 1)
    same = (row // C) == (col // C)
    tr, tc = row % C, col % C
    before = ((dv == 0) & (tc < tr)) | ((dv != 0) & (tc > tr))
    strict = same & before
    incl = same & (before | (tc == tr))
    eye = row == col
    cum_mask = jnp.where(incl[:C, :C], 1.0, 0.0).astype(BF16)
    lane_lo = lax.broadcasted_iota(jnp.int32, (C, 128), 1) < HEAD

    def stack(x):
        return jnp.concatenate([jnp.where(lane_lo, x, 0.0), jnp.where(lane_lo, 0.0, x)], axis=0)

    def unstack(z):
        return z[:C] + z[C:]

    def group(ps):
        def each(fn, *lists):
            return [fn(*xs) for xs in zip(*lists)]

        lam = [lam_ref[0, 0, p] for p in ps]
        kap = [kap_ref[0, p].astype(F32) for p in ps]
        v_z = [stack(v_ref[0, p].astype(F32)).astype(BF16) for p in ps]
        a = [a_ref[0, 0, p].astype(F32) for p in ps]
        kd = [k_ref[0, p].astype(F32) * (1.0 + (ai - 1.0) * ka_ref[p]) for p, ai in zip(ps, a)]
        bet = each(lambda x, y: x * y, kap, a)

        def prefix(l):
            hi = l.astype(BF16)
            lo = (l - hi.astype(F32)).astype(BF16)
            cs = jnp.dot(cum_mask, jnp.concatenate([hi, lo], axis=1), preferred_element_type=F32)
            return cs[:, :128] + cs[:, 128:]

        lc = each(prefix, lam)
        ltot = [jnp.sum(l, axis=0, keepdims=True) for l in lam]
        e_neg = [jnp.exp(-x) for x in lc]
        qk_z = each(lambda kp, c, l: stack(kp * jnp.exp(c - l)).astype(BF16), kap, lc, lam)
        rt_z = [stack(r_ref[0, p].astype(F32) * jnp.exp(c)) for p, c in zip(ps, lc)]

        def scores(q, rz, kdi, bi, en):
            kt = (kdi * en).astype(BF16)
            bt = (bi * en).astype(BF16)
            return _dot_nt(jnp.concatenate([q, rz.astype(BF16)], axis=0), jnp.concatenate([kt, kt, bt, bt], axis=0))

        sc = each(scores, qk_z, rt_z, kd, bet, e_neg)
        m_k = [jnp.where(strict, s[:128, :128], 0.0).astype(BF16) for s in sc]
        a_kb = [jnp.concatenate([jnp.where(incl, s[128:, :128], 0.0), jnp.where(incl, -s[128:, 128:], 0.0)],
                                axis=1).astype(BF16) for s in sc]
        mkv = each(_dot, m_k, v_z)

        levels = C.bit_length() - 1
        pw = [jnp.where(strict, -s[:128, 128:], 0.0) for s in sc]
        t_inv = [jnp.where(eye, 1.0, 0.0) + x for x in pw]
        pw = [_dot(x, x) for x in pw]
        for _ in range(1, levels - 1):
            both = each(lambda x, t: _dot(x, jnp.concatenate([x, t], axis=1)), pw, t_inv)
            pw = [x[:, :128] for x in both]
            t_inv = each(lambda t, x: t + x[:, 128:], t_inv, both)
        t_inv = each(lambda t, x: t + _dot(x, t), t_inv, pw)

        tw = each(lambda t, q, m: _dot(t, jnp.concatenate([q, m.astype(BF16)], axis=1)).astype(BF16),
                  t_inv, qk_z, mkv)
        rhs = each(lambda vz, t: jnp.concatenate(
            [jnp.concatenate([vz, jnp.zeros_like(vz)], axis=1),
             jnp.concatenate([t[:, 128:], t[:, :128]], axis=1)], axis=0), v_z, tw)
        yr = each(_dot, a_kb, rhs)
        e_tail = each(lambda lt, c: jnp.exp(lt - c), ltot, lc)
        khb = each(lambda x, b, e: jnp.concatenate([stack(x * e), stack(-b * e)], axis=0), kd, bet, e_tail)
        hg = each(_dot_tn, khb, rhs)
        rhat = each(lambda rz, x: unstack(rz + x[:, 128:]), rt_z, yr)
        g = each(lambda lt, x: jnp.where(eye, jnp.exp(lt), 0.0) + x[:, 128:], ltot, hg)

        def advance(p, gi, rh):
            s0 = s_ref[p]
            s_hi = s0.astype(BF16)
            s_lo = (s0 - s_hi.astype(F32)).astype(BF16)
            gs = _dot(jnp.concatenate([gi, rh], axis=0), jnp.concatenate([s_hi, s_lo], axis=1))
            return gs[:, :128] + gs[:, 128:]

        gs = each(advance, ps, g, rhat)
        for p, x, h, yl in zip(ps, gs, hg, yr):
            s_ref[p] = x[:128] + h[:, :128]
            y_ref[0, 0, p] = (x[128:] + unstack(yl[:, :128])).astype(y_ref.dtype)

    for first in range(0, PAIRS, WKV_GROUP):
        group(range(first, first + WKV_GROUP))


def _wkv_call(k, kap, v, r, a, lam, ka, ctx_len, out_dtype=BF16):
    B, _, T, _ = k.shape
    C = WKV_C
    ns = T // C
    nc = ctx_len // C

    def chunk(d, j):
        rev = jnp.where(j < nc, nc - 1 - j, ns - 1 + nc - j)
        return jnp.where(d == 0, j, rev)

    tok = pl.BlockSpec((1, PAIRS, C, 128), lambda b, d, j: (b, 0, chunk(d, j), 0))
    per_dir = pl.BlockSpec((1, 1, PAIRS, C, 128), lambda b, d, j: (d, b, 0, chunk(d, j), 0))
    return pl.pallas_call(
        _wkv_kernel,
        out_shape=jax.ShapeDtypeStruct((2, B, PAIRS, T, 128), out_dtype),
        grid=(B, 2, ns),
        in_specs=[tok, tok, tok, tok, per_dir, per_dir, _full((PAIRS, 1, 128))],
        out_specs=per_dir,
        scratch_shapes=[pltpu.VMEM((PAIRS, 128, 128), F32)],
        compiler_params=_cparams(("parallel", "parallel", "arbitrary")),
        name="wkv_chunked",
    )(k, kap, v, r, a, lam, ka)


def _mixer_tail(x, mix_out, m, n2g, wrt, x_o, h_o, lg_o):
    x_new = x + m[2:3] * mix_out
    h2 = _adanorm(x_new, n2g[...], m[3:4], m[4:5])
    x_o[0] = x_new
    h_o[0] = h2.astype(h_o.dtype)
    lg_o[0] = lax.dot_general(wrt[...], h2, (((1,), (1,)), ((), ())), preferred_element_type=F32)


def _tail_out(B, nt):
    t_out = nt * TM
    shapes = [jax.ShapeDtypeStruct((B, t_out, D), F32), jax.ShapeDtypeStruct((B, t_out, D), BF16),
              jax.ShapeDtypeStruct((B, N_EXPERTS, t_out), F32)]
    tile = pl.BlockSpec((1, TM, D), lambda b, j: (b, j, 0))
    specs = [tile, tile, pl.BlockSpec((1, N_EXPERTS, TM), lambda b, j: (b, 0, j))]
    return shapes, specs


def _rwkv_post_kernel(y0, y1, bon, g, x, modc, lng, lnb, wo, n2g, wrt, ep, x_o, h_o, lg_o):
    parts = []
    for p in range(PAIRS):
        y = y0[0, 0, p].astype(F32) + y1[0, 0, p].astype(F32)
        dlt = y - _dot(y, ep[...]) * (1.0 / HEAD)
        var = _dot(dlt * dlt, ep[...]) * (1.0 / HEAD)
        parts.append(dlt * lax.rsqrt(var + GN_EPS))
    yn = jnp.concatenate(parts, axis=1)
    z = (yn * lng[...] + lnb[...] + bon[0].astype(F32)) * g[0].astype(F32)
    _mixer_tail(x[0], _dot(z, wo[...]), modc[0, 0], n2g, wrt, x_o, h_o, lg_o)


def _rwkv_post_call(y, bonus, g, x, mods, layer, prm, n2g, wrt, ctx_out):
    B, T, _ = x.shape
    nt = T // TM
    off = 0 if ctx_out else 1
    tile = pl.BlockSpec((1, TM, D), lambda b, j: (b, j + off, 0))
    ydir = [pl.BlockSpec((1, 1, PAIRS, TM, 128), functools.partial(lambda b, j, d: (d, b, 0, j + off, 0), d=d))
            for d in range(2)]
    consts = [prm["lng"], prm["lnb"], prm["wo"], n2g, wrt, prm["epair"]]
    shapes, ospecs = _tail_out(B, nt - off)
    return pl.pallas_call(
        _rwkv_post_kernel, out_shape=shapes, grid=(B, nt - off),
        in_specs=ydir + [tile, tile, tile, _mod_spec(layer, B, ctx_out)] + [_full(a.shape) for a in consts],
        out_specs=ospecs,
        compiler_params=_cparams(("parallel", "parallel")),
        name="rwkv_post",
    )(y, y, bonus, g, x, mods, *consts)


def _sgu_kernel(x, f, modp, modc, n1g, win, bin_, lng, lnb, ws, bs, wout, n2g, wrt, x_o, h_o, lg_o):
    x1 = x[0] + modp[0, 0, 5:6, :] * f[0]
    m = modc[0, 0]
    h = _adanorm(x1, n1g[...], m[0:1], m[1:2])
    z = _dot(h, win[...]) + bin_[...]
    z = 0.5 * z * (1.0 + lax.erf(z * 0.7071067811865476))
    u, v = z[:, :SG_HALF], z[:, SG_HALF:]
    mu = jnp.mean(v, axis=-1, keepdims=True)
    dv = v - mu
    var = jnp.mean(dv * dv, axis=-1, keepdims=True)
    vn = (dv * lax.rsqrt(var + LN_EPS) * lng[...] + lnb[...]).astype(BF16)
    gw = SG_HALF // SG_GROUPS
    rows = []
    for c in range(TM // SG_CHUNK):
        rs = slice(c * SG_CHUNK, (c + 1) * SG_CHUNK)
        cols = []
        for gi in range(SG_GROUPS):
            cs = slice(gi * gw, (gi + 1) * gw)
            vs = jnp.dot(ws[gi], vn[rs, cs], preferred_element_type=F32) + bs[gi]
            cols.append((u[rs, cs] * vs).astype(BF16))
        rows.append(jnp.concatenate(cols, axis=1))
    gated = jnp.concatenate(rows, axis=0)
    _mixer_tail(x1, _dot(gated, wout[...]), m, n2g, wrt, x_o, h_o, lg_o)


def _sgu_call(x, f, mods, layer, prm, n2g, wrt, has_ctx):
    B, T, _ = x.shape
    nt = T // TM
    tile = pl.BlockSpec((1, TM, D), lambda b, j: (b, j, 0))
    consts = [prm["n1g"], prm["win"], prm["bin"], prm["lng"], prm["lnb"], prm["ws"], prm["bs"], prm["wout"],
              n2g, wrt]
    shapes, ospecs = _tail_out(B, nt)
    return pl.pallas_call(
        _sgu_kernel, out_shape=shapes, grid=(B, nt),
        in_specs=[tile, tile, _mod_spec(layer - 1, B, has_ctx), _mod_spec(layer, B, has_ctx)]
        + [_full(a.shape) for a in consts],
        out_specs=ospecs,
        compiler_params=_cparams(("parallel", "parallel")),
        name="chunk_sgu",
    )(x, f, mods, mods, *consts)


def _route_kernel(lg_ref, rb_ref, slot_ref, wt_ref, meta_ref, *, W):
    E, G = N_EXPERTS, N_GROUPS
    per = E // G
    s = _sigmoid(lg_ref[0])
    biased = s + rb_ref[:, 0:1]
    eidx = lax.broadcasted_iota(jnp.int32, (E, W), 0)

    best = None
    for gi in range(G):
        a, b, c, d = (biased[per * gi + i:per * gi + i + 1, :] for i in range(per))
        hi1, lo1, hi2, lo2 = jnp.maximum(a, b), jnp.minimum(a, b), jnp.maximum(c, d), jnp.minimum(c, d)
        top2 = jnp.maximum(hi1, hi2) + jnp.maximum(jnp.minimum(hi1, hi2), jnp.maximum(lo1, lo2))
        if best is None:
            best, bg = top2, jnp.zeros((1, W), jnp.int32)
        else:
            upd = top2 > best
            bg = jnp.where(upd, gi, bg)
            best = jnp.where(upd, top2, best)
    neg = -jnp.inf
    m1 = jnp.where((eidx // per) == bg, biased, neg)
    i1 = jnp.min(jnp.where(m1 == jnp.max(m1, axis=0, keepdims=True), eidx, E), axis=0, keepdims=True)
    sel1 = eidx == i1
    m2 = jnp.where(sel1, neg, m1)
    i2 = jnp.min(jnp.where(m2 == jnp.max(m2, axis=0, keepdims=True), eidx, E), axis=0, keepdims=True)
    sel2 = eidx == i2
    w1 = jnp.sum(jnp.where(sel1, s, 0.0), axis=0, keepdims=True)
    w2 = jnp.sum(jnp.where(sel2, s, 0.0), axis=0, keepdims=True)
    wt_ref[0] = jnp.concatenate([w1 / (w1 + w2), w2 / (w1 + w2)], axis=0)

    onehot = jnp.where(sel1 | sel2, 1.0, 0.0)
    blk = 128
    upper = jnp.where(lax.broadcasted_iota(jnp.int32, (blk, blk), 0) < lax.broadcasted_iota(jnp.int32, (blk, blk), 1),
                      1.0, 0.0).astype(BF16)
    carry = jnp.zeros((E, 1), F32)
    ranks = []
    for i in range(W // blk):
        ob = onehot[:, i * blk:(i + 1) * blk]
        ranks.append(jnp.dot(ob.astype(BF16), upper, preferred_element_type=F32) + carry)
        carry = carry + jnp.sum(ob, axis=1, keepdims=True)
    rank = jnp.concatenate(ranks, axis=1)
    padded = jnp.ceil(carry * (1.0 / MOE_TM)) * MOE_TM
    offs, cum = [], jnp.zeros((1, 1), F32)
    lane_tile = lax.broadcasted_iota(jnp.int32, (1, 128), 1).astype(F32) * MOE_TM
    texp = jnp.zeros((1, 128), jnp.int32)
    e_last = jnp.zeros((1, 1), jnp.int32)
    for e in range(E):
        offs.append(cum)
        pe = padded[e:e + 1, :]
        cum = cum + pe
        texp = texp + jnp.where(cum <= lane_tile, 1, 0)
        e_last = jnp.where(pe > 0.0, e, e_last)
    pos = jnp.concatenate(offs, axis=0) + rank
    slot1 = jnp.sum(jnp.where(sel1, pos, 0.0), axis=0, keepdims=True)
    slot2 = jnp.sum(jnp.where(sel2, pos, 0.0), axis=0, keepdims=True)
    slot_ref[0] = jnp.concatenate([slot1, slot2], axis=0).astype(jnp.int32)
    n_tiles = (cum * (1.0 / MOE_TM)).astype(jnp.int32)
    meta_ref[0] = jnp.concatenate([jnp.minimum(texp, e_last), jnp.zeros((1, 128), jnp.int32) + n_tiles,
                                   jnp.zeros((6, 128), jnp.int32)], axis=0)


def _route_call(logits, rbias, n_split):
    B, _, T = logits.shape
    W = T // n_split
    nw = B * n_split
    return pl.pallas_call(
        functools.partial(_route_kernel, W=W),
        out_shape=[jax.ShapeDtypeStruct((nw, 2, W), jnp.int32), jax.ShapeDtypeStruct((nw, 2, W), F32),
                   jax.ShapeDtypeStruct((nw, 8, 128), jnp.int32)],
        grid=(nw,),
        in_specs=[pl.BlockSpec((1, N_EXPERTS, W), lambda w: (w // n_split, 0, w % n_split)), _full(rbias.shape)],
        out_specs=[pl.BlockSpec((1, 2, W), lambda w: (w, 0, 0)), pl.BlockSpec((1, 2, W), lambda w: (w, 0, 0)),
                   pl.BlockSpec((1, 8, 128), lambda w: (w, 0, 0))],
        compiler_params=_cparams(("parallel",)),
        name="moe_route",
    )(logits, rbias)


def _moe_kernel(texp_ref, nt_ref, h_ref, slot_ref, col_ref, wgu_ref, wd_ref, o_ref, *, W):
    w = pl.program_id(0)
    t = pl.program_id(1)

    @pl.when(t == 0)
    def _():
        o_ref[...] = jnp.zeros_like(o_ref)

    @pl.when(t < nt_ref[w])
    def _():
        base = t * MOE_TM
        rid = lax.broadcasted_iota(jnp.int32, (MOE_TM, W), 0) + base
        hit = (slot_ref[0, 0:1, :] == rid) | (slot_ref[0, 1:2, :] == rid)
        disp = jnp.where(hit, 1.0, 0.0).astype(BF16)
        xt = jnp.dot(disp, h_ref[0], preferred_element_type=F32).astype(BF16)
        gu = jnp.dot(xt, wgu_ref[0], preferred_element_type=F32)
        gate, up = gu[:, :EXPERT_FFN], gu[:, EXPERT_FFN:]
        y = _dot(gate * _sigmoid(gate) * up, wd_ref[0]).astype(BF16)
        cid = (lax.broadcasted_iota(jnp.int32, (W, MOE_TM), 1) + base).astype(F32)
        col = col_ref[0]
        comb = (jnp.where(col[:, 0:1] == cid, col[:, 2:3], 0.0)
                + jnp.where(col[:, 1:2] == cid, col[:, 3:4], 0.0)).astype(BF16)
        o_ref[0] += jnp.dot(comb, y, preferred_element_type=F32)


def _moe_call(h2, slots, colinfo, texp, ntiles, wgu, wd):
    nw, W, _ = h2.shape
    max_tiles = texp.shape[0] // nw
    win = lambda w, t, te, nt: (w, 0, 0)
    expert = lambda w, t, te, nt: (te[w * max_tiles + t], 0, 0)
    grid_spec = pltpu.PrefetchScalarGridSpec(
        num_scalar_prefetch=2, grid=(nw, max_tiles),
        in_specs=[pl.BlockSpec((1, W, D), win), pl.BlockSpec((1, 2, W), win), pl.BlockSpec((1, W, 4), win),
                  pl.BlockSpec((1, D, 2 * EXPERT_FFN), expert), pl.BlockSpec((1, EXPERT_FFN, D), expert)],
        out_specs=pl.BlockSpec((1, W, D), win))
    return pl.pallas_call(
        functools.partial(_moe_kernel, W=W),
        out_shape=jax.ShapeDtypeStruct((nw, W, D), F32),
        grid_spec=grid_spec,
        compiler_params=_cparams(("parallel", "arbitrary")),
        name="moe_experts",
    )(texp, ntiles, h2, slots, colinfo, wgu, wd)


def _moe_layer(h2, logits, rbias, wgu, wd, n_split=1):
    B, T, _ = h2.shape
    window = T // n_split
    slots, wts, meta = _route_call(logits, rbias, n_split)
    max_tiles = 2 * window // MOE_TM + N_EXPERTS
    texp = meta[:, 0, :max_tiles].reshape(-1)
    ntiles = meta[:, 1, 0]
    colinfo = jnp.concatenate([slots.astype(F32), wts], axis=1).transpose(0, 2, 1)
    f = _moe_call(h2.reshape(B * n_split, window, D), slots, colinfo, texp, ntiles, wgu, wd)
    return f.reshape(B, T, D)


def _final_kernel(x, f, modp, g, o):
    o[0] = _rmsnorm(x[0] + modp[0, 0, 5:6, :] * f[0]) * g[...]


def _final_call(x, f, mods, layer, final_g):
    B, T, _ = x.shape
    tile = pl.BlockSpec((1, TM, D), lambda b, j: (b, j, 0))
    return pl.pallas_call(
        _final_kernel, out_shape=jax.ShapeDtypeStruct((B, T, D), F32), grid=(B, T // TM),
        in_specs=[tile, tile, _mod_spec(layer, B, False), _full(final_g.shape)], out_specs=tile,
        compiler_params=_cparams(("parallel", "parallel")),
        name="final_norm",
    )(x, f, mods, final_g)


def _rwkv_params(a, norm1_g, layer, rw_mu, rw_wr, rw_wk, rw_wv, rw_wo, rw_w0, rw_w1, rw_w2, rw_a0, rw_a1, rw_a2,
                 rw_g1, rw_g2, rw_kk, rw_ka, rw_rk, rw_lnx_g, rw_lnx_b, epair):
    lora = rw_w1.shape[-1]

    def block_diag(w2):
        z = jnp.zeros((lora, D), w2.dtype)
        return jnp.concatenate([jnp.concatenate([w2[0], z], axis=1), jnp.concatenate([z, w2[1]], axis=1)], axis=0)

    return dict(
        n1g=norm1_g[layer].reshape(1, D), mu=rw_mu[a],
        wr=rw_wr[a].astype(BF16), wk=rw_wk[a].astype(BF16), wv=rw_wv[a].astype(BF16), wo=rw_wo[a].astype(BF16),
        w1c=jnp.concatenate([rw_w1[a, 0], rw_w1[a, 1]], axis=1).astype(BF16), w2b=block_diag(rw_w2[a]).astype(BF16),
        w0=rw_w0[a],
        a1c=jnp.concatenate([rw_a1[a, 0], rw_a1[a, 1]], axis=1).astype(BF16), a2b=block_diag(rw_a2[a]).astype(BF16),
        a0=rw_a0[a],
        g1=rw_g1[a].astype(BF16), g2=rw_g2[a].astype(BF16),
        kk=rw_kk[a].reshape(1, D), ka=rw_ka[a].reshape(1, D), rk=rw_rk[a].reshape(1, D),
        lng=rw_lnx_g[a].reshape(1, D), lnb=rw_lnx_b[a].reshape(1, D), epair=epair)


def kernel(x, c, ctx, c_ctx, w_mod, b_mod, norm1_g, norm2_g, rw_mu, rw_wr, rw_wk, rw_wv, rw_wo, rw_w0, rw_w1, rw_w2, rw_a0, rw_a1, rw_a2, rw_v0, rw_v1, rw_v2, rw_g1, rw_g2, rw_kk, rw_ka, rw_rk, rw_lnx_g, rw_lnx_b, sg_w_in, sg_b_in, sg_ln_g, sg_ln_b, sg_w_s, sg_b_s, sg_w_out, w_router, router_bias, ex_w_gate, ex_w_up, ex_w_down, final_g):
    B, L, _ = x.shape
    Lc = ctx.shape[1]
    assert Lc == TM and L % TM == 0 and B + 1 <= MOD_ROWS and w_mod.shape[0] == 4

    cvec = jnp.concatenate([c, c_ctx[None, :], jnp.zeros((MOD_ROWS - B - 1, D), F32)], axis=0)
    mods = _mod_call(cvec, w_mod, b_mod).reshape(w_mod.shape[0], MOD_ROWS, N_MOD, D)
    xs = jnp.concatenate([ctx, x], axis=1)

    lane = jnp.arange(128) // HEAD
    epair = (lane[:, None] == lane[None, :]).astype(BF16)
    wrt = w_router.T
    rbias = jnp.broadcast_to(router_bias[:, None], (N_EXPERTS, 128))
    n2g = [norm2_g[i].reshape(1, D) for i in range(4)]
    wgu = [jnp.concatenate([ex_w_gate[i], ex_w_up[i]], axis=-1).astype(BF16) for i in range(4)]
    wd = [ex_w_down[i].astype(BF16) for i in range(4)]
    rw_common = (rw_mu, rw_wr, rw_wk, rw_wv, rw_wo, rw_w0, rw_w1, rw_w2, rw_a0, rw_a1, rw_a2, rw_g1, rw_g2,
                 rw_kk, rw_ka, rw_rk, rw_lnx_g, rw_lnx_b, epair)

    def sgu_params(b, layer):
        return dict(n1g=norm1_g[layer].reshape(1, D), win=sg_w_in[b].astype(BF16), bin=sg_b_in[b].reshape(1, -1),
                    lng=sg_ln_g[b].reshape(1, -1), lnb=sg_ln_b[b].reshape(1, -1), ws=sg_w_s[b].astype(BF16),
                    bs=jnp.broadcast_to(sg_b_s[b][:, :, None], (SG_GROUPS, SG_CHUNK, SG_HALF // SG_GROUPS)),
                    wout=sg_w_out[b].astype(BF16))

    p0 = _rwkv_params(0, norm1_g, 0, *rw_common)
    k0, kap0, v0, r0, a0, lam0, g0, bon0 = _rwkv_pre_call(xs, None, mods, 0, p0, None, None)
    y0 = _wkv_call(k0, kap0, v0, r0, a0, lam0, p0["ka"].reshape(PAIRS, 1, 128), Lc)
    x1, h1, lg1 = _rwkv_post_call(y0, bon0, g0, xs, mods, 0, p0, n2g[0], wrt, True)
    f0 = _moe_layer(h1, lg1, rbias, wgu[0], wd[0])

    x2, h2, lg2 = _sgu_call(x1, f0, mods, 1, sgu_params(0, 1), n2g[1], wrt, True)
    f1 = _moe_layer(h2, lg2, rbias, wgu[1], wd[1])

    p2 = _rwkv_params(1, norm1_g, 2, *rw_common)
    vres = dict(v0=rw_v0[0].reshape(1, D),
                v1=jnp.pad(rw_v1[0], ((0, 0), (0, 128 - rw_v1.shape[-1]))).astype(BF16),
                v2=jnp.pad(rw_v2[0], ((0, 128 - rw_v2.shape[-2]), (0, 0))).astype(BF16))
    k2, kap2, v2, r2, a2, lam2, g2, bon2, x2r = _rwkv_pre_call(x2, f1, mods, 2, p2, vres, v0)
    y2 = _wkv_call(k2, kap2, v2, r2, a2, lam2, p2["ka"].reshape(PAIRS, 1, 128), Lc)
    x3, h3, lg3 = _rwkv_post_call(y2, bon2, g2, x2r, mods, 2, p2, n2g[2], wrt, False)
    f2 = _moe_layer(h3, lg3, rbias, wgu[2], wd[2])

    x4, h4, lg4 = _sgu_call(x3, f2, mods, 3, sgu_params(1, 3), n2g[3], wrt, False)
    f3 = _moe_layer(h4, lg4, rbias, wgu[3], wd[3])
    return _final_call(x4, f3, mods, 3, final_g.reshape(1, D))
```

```python
import functools

import jax
import jax.numpy as jnp
from jax import lax
from jax.experimental import pallas as pl
from jax.experimental.pallas import tpu as pltpu

F32 = jnp.float32
BF16 = jnp.bfloat16

D = 1024
HEAD = 64
PAIRS = D // 128
GRID_W = 64
N_MOD = 6
N_EXPERTS = 16
N_GROUPS = 4
EXPERT_FFN = 512
SG_CHUNK = 128
SG_GROUPS = 16
SG_HALF = 2048
NORM_EPS = 1e-6
LN_EPS = 1e-5
GN_EPS = 64e-5

TM = 256
HALO = GRID_W
WKV_C = 64
WKV_GROUP = 8
WKV_SUB = 2
MOE_TM = 256
MOE_BLOCK = 256
MOE_G = 16
MOE_CAP = 2 * MOE_BLOCK + N_EXPERTS * (MOE_G - 1)
MOD_ROWS = 40
VMEM_LIMIT = 56 * 1024 * 1024


def _cparams(sem):
    return pltpu.CompilerParams(dimension_semantics=sem, vmem_limit_bytes=VMEM_LIMIT)


def _dot(a, b):
    return jnp.dot(a.astype(BF16), b.astype(BF16), preferred_element_type=F32)


def _dot_nt(a, b):
    return lax.dot_general(a.astype(BF16), b.astype(BF16), (((1,), (1,)), ((), ())),
                           preferred_element_type=F32)


def _dot_tn(a, b):
    return lax.dot_general(a.astype(BF16), b.astype(BF16), (((0,), (0,)), ((), ())),
                           preferred_element_type=F32)


def _sigmoid(x):
    return 1.0 / (1.0 + jnp.exp(-x))


def _rmsnorm(x):
    return x * lax.rsqrt(jnp.mean(x * x, axis=-1, keepdims=True) + NORM_EPS)


def _adanorm(x, g, shift, scale):
    return _rmsnorm(x) * g * (1.0 + scale) + shift


def _full(shape):
    zeros = (0,) * len(shape)
    return pl.BlockSpec(shape, lambda *_: zeros)


def _mod_kernel(c_ref, w_ref, b_ref, o_ref):
    c = c_ref[...]
    o_ref[0] = jnp.dot(c * _sigmoid(c), w_ref[0], preferred_element_type=F32) + b_ref[0]


def _mod_call(cvec, w_mod, b_mod):
    depth = w_mod.shape[0]
    return pl.pallas_call(
        _mod_kernel,
        out_shape=jax.ShapeDtypeStruct((depth, MOD_ROWS, N_MOD * D), F32),
        grid=(depth, N_MOD),
        in_specs=[pl.BlockSpec((MOD_ROWS, D), lambda i, n: (0, 0)),
                  pl.BlockSpec((1, D, D), lambda i, n: (i, 0, n)),
                  pl.BlockSpec((1, 1, D), lambda i, n: (i, 0, n))],
        out_specs=pl.BlockSpec((1, MOD_ROWS, D), lambda i, n: (i, 0, n)),
        compiler_params=_cparams(("parallel", "parallel")),
        name="modulation",
    )(cvec, w_mod, b_mod.reshape(depth, 1, N_MOD * D))


def _mod_spec(layer, batch, ctx_tile):
    if ctx_tile:
        return pl.BlockSpec((1, 1, N_MOD, D), lambda b, j: (layer, jnp.where(j == 0, batch, b), 0, 0))
    return pl.BlockSpec((1, 1, N_MOD, D), lambda b, j: (layer, b, 0, 0))


def _rwkv_pre_kernel(*refs, has_f, has_vres, n_tiles):
    it = iter(refs)
    xm, xp, xn = next(it), next(it), next(it)
    if has_f:
        fm, fp, fn, modp = next(it), next(it), next(it), next(it)
    modc, n1g, mu = next(it), next(it), next(it)
    wr, wk, wv = next(it), next(it), next(it)
    w1c, w2b, w0 = next(it), next(it), next(it)
    a1c, a2b, a0 = next(it), next(it), next(it)
    g1, g2 = next(it), next(it)
    if has_vres:
        v0, v1, v2, vf = next(it), next(it), next(it), next(it)
    kkv, kav, rkv, ep = next(it), next(it), next(it), next(it)
    k_o, kap_o, v_o, r_o, a_o, lam_o, g_o, bon_o = (next(it) for _ in range(8))
    if has_f:
        x_o = next(it)

    j = pl.program_id(1)
    x_all = jnp.concatenate([xp[0], xm[0], xn[0]], axis=0)
    if has_f:
        f_all = jnp.concatenate([fp[0], fm[0], fn[0]], axis=0)
        x_all = x_all + modp[0, 0, 5:6, :] * f_all
        x_o[0] = x_all[HALO:HALO + TM]
    m = modc[0, 0]
    h_all = _adanorm(x_all, n1g[...], m[0:1], m[1:2])
    hm = h_all[HALO:HALO + TM]

    q = D // 4
    r = lax.broadcasted_iota(jnp.int32, (TM, q), 0)
    jv = jnp.zeros((TM, q), jnp.int32) + j
    is_ctx = jv == 0
    colp = r % GRID_W
    prev = pltpu.roll(hm, 1, axis=0)
    nxt = pltpu.roll(hm, TM - 1, axis=0)
    up = h_all[0:TM]
    down = h_all[2 * HALO:2 * HALO + TM]
    first, last = r == 0, r == TM - 1
    q0 = jnp.where((is_ctx & first) | (~is_ctx & (colp == 0)), 0.0, prev[:, 0:q])
    q1 = jnp.where(is_ctx, jnp.where(first, 0.0, prev[:, q:2 * q]),
                   jnp.where(colp == GRID_W - 1, 0.0, nxt[:, q:2 * q]))
    q2 = jnp.where(is_ctx, jnp.where(last, 0.0, nxt[:, 2 * q:3 * q]),
                   jnp.where((jv == 1) & (r < HALO), 0.0, up[:, 2 * q:3 * q]))
    q3 = jnp.where(is_ctx, jnp.where(last, 0.0, nxt[:, 3 * q:]),
                   jnp.where((jv == n_tiles - 1) & (r >= TM - HALO), 0.0, down[:, 3 * q:]))
    xx = jnp.concatenate([q0, q1, q2, q3], axis=1) - hm

    def mix(i):
        return (hm + xx * mu[i:i + 1, :]).astype(BF16)

    xr, xw, xk, xv, xa, xg = (mix(i) for i in range(6))
    rr = _dot(xr, wr[...])
    kk = _dot(xk, wk[...])
    vv = _dot(xv, wv[...])
    if has_vres:
        gate = _sigmoid(v0[...] + _dot(_dot(xv, v1[...]), v2[...]))
        vfirst = jnp.concatenate([vf[0, p].astype(F32) for p in range(PAIRS)], axis=1)
        vv = vv + (vfirst - vv) * gate
    g_o[0] = _dot(_sigmoid(_dot(xg, g1[...])), g2[...]).astype(g_o.dtype)

    zw = _dot(jnp.tanh(_dot(xw, w1c[...])), w2b[...])
    za = _dot(_dot(xa, a1c[...]), a2b[...])
    ka = kav[...]
    kd_sum = jnp.zeros((TM, D), F32)
    for d in range(2):
        z = w0[d:d + 1, :] + zw[:, d * D:(d + 1) * D]
        softplus = jnp.maximum(-z, 0.0) + jnp.log(1.0 + jnp.exp(-jnp.abs(z)))
        lam = -jnp.exp(-softplus - 0.5)
        a = _sigmoid(a0[d:d + 1, :] + za[:, d * D:(d + 1) * D])
        kd_sum = kd_sum + kk * (1.0 + (a - 1.0) * ka)
        for p in range(PAIRS):
            sl = slice(128 * p, 128 * (p + 1))
            lam_o[d, 0, p] = lam[:, sl]
            a_o[d, 0, p] = a[:, sl].astype(a_o.dtype)

    kkk = kk * kkv[...]
    rk = rr * rkv[...] * kd_sum
    for p in range(PAIRS):
        sl = slice(128 * p, 128 * (p + 1))
        kp = kkk[:, sl]
        ss = _dot(kp * kp, ep[...])
        kap_o[0, p] = (kp * lax.rsqrt(jnp.maximum(ss, 1e-24))).astype(kap_o.dtype)
        bon_o[0, :, sl] = (_dot(rk[:, sl], ep[...]) * vv[:, sl]).astype(bon_o.dtype)
        k_o[0, p] = kk[:, sl].astype(k_o.dtype)
        v_o[0, p] = vv[:, sl].astype(v_o.dtype)
        r_o[0, p] = rr[:, sl].astype(r_o.dtype)


def _rwkv_pre_call(xs, f_prev, mods, layer, prm, vres, v_first):
    B, T, _ = xs.shape
    nt = T // TM
    nh = T // HALO
    has_f = f_prev is not None
    has_vres = vres is not None
    main = pl.BlockSpec((1, TM, D), lambda b, j: (b, j, 0))
    prev = pl.BlockSpec((1, HALO, D), lambda b, j: (b, jnp.maximum(j * (TM // HALO) - 1, 0), 0))
    nxt = pl.BlockSpec((1, HALO, D), lambda b, j: (b, jnp.minimum((j + 1) * (TM // HALO), nh - 1), 0))
    pm = pl.BlockSpec((1, PAIRS, TM, 128), lambda b, j: (b, 0, j, 0))
    pm2 = pl.BlockSpec((2, 1, PAIRS, TM, 128), lambda b, j: (0, b, 0, j, 0))

    args, specs = [xs, xs, xs], [main, prev, nxt]
    if has_f:
        args += [f_prev, f_prev, f_prev, mods]
        specs += [main, prev, nxt, _mod_spec(layer - 1, B, True)]
    args += [mods, prm["n1g"], prm["mu"], prm["wr"], prm["wk"], prm["wv"], prm["w1c"], prm["w2b"], prm["w0"],
             prm["a1c"], prm["a2b"], prm["a0"], prm["g1"], prm["g2"]]
    specs += [_mod_spec(layer, B, True)] + [_full(a.shape) for a in args[len(specs) + 1:]]
    if has_vres:
        args += [vres["v0"], vres["v1"], vres["v2"], v_first]
        specs += [_full(vres["v0"].shape), _full(vres["v1"].shape), _full(vres["v2"].shape), pm]
    tail = [prm["kk"], prm["ka"], prm["rk"], prm["epair"]]
    args += tail
    specs += [_full(a.shape) for a in tail]

    pm_shape = jax.ShapeDtypeStruct((B, PAIRS, T, 128), BF16)
    out_shape = [pm_shape, pm_shape, pm_shape, pm_shape,
                 jax.ShapeDtypeStruct((2, B, PAIRS, T, 128), BF16),
                 jax.ShapeDtypeStruct((2, B, PAIRS, T, 128), F32),
                 jax.ShapeDtypeStruct((B, T, D), BF16),
                 jax.ShapeDtypeStruct((B, T, D), BF16)]
    out_specs = [pm, pm, pm, pm, pm2, pm2, main, main]
    if has_f:
        out_shape.append(jax.ShapeDtypeStruct((B, T, D), F32))
        out_specs.append(main)
    return pl.pallas_call(
        functools.partial(_rwkv_pre_kernel, has_f=has_f, has_vres=has_vres, n_tiles=nt),
        out_shape=out_shape, grid=(B, nt), in_specs=specs, out_specs=out_specs,
        compiler_params=_cparams(("parallel", "parallel")),
        name="rwkv_pre",
    )(*args)


def _wkv_kernel(k_ref, kap_ref, v_ref, r_ref, a_ref, lam_ref, ka_ref, y_ref, s_ref):
    C = WKV_C
    d = pl.program_id(1)
    j = pl.program_id(2)

    @pl.when(j == 0)
    def _():
        s_ref[...] = jnp.zeros_like(s_ref)

    dv = jnp.zeros((128, 128), jnp.int32) + d
    row = lax.broadcasted_iota(jnp.int32, (128, 128), 0)
    col = lax.broadcasted_iota(jnp.int32, (128, 128), 1)
    same = (row // C) == (col // C)
    tr, tc = row % C, col % C
    before = ((dv == 0) & (tc < tr)) | ((dv != 0) & (tc > tr))
    strict = same & before
    incl = same & (before | (tc == tr))
    eye = row == col
    cum_mask = jnp.where(incl[:C, :C], 1.0, 0.0).astype(BF16)
    lane_lo = lax.broadcasted_iota(jnp.int32, (C, 128), 1) < HEAD

    def stack(x):
        return jnp.concatenate([jnp.where(lane_lo, x, 0.0), jnp.where(lane_lo, 0.0, x)], axis=0)

    def unstack(z):
        return z[:C] + z[C:]

    offs = [pl.multiple_of(jnp.where(d == 0, q * C, (WKV_SUB - 1 - q) * C), C) for q in range(WKV_SUB)]

    def group(ps):
        def each(fn, *lists):
            return [fn(*xs) for xs in zip(*lists)]

        inst = [(p, q) for q in range(WKV_SUB) for p in ps]
        rows = [pl.ds(offs[q], C) for _, q in inst]
        lam = [lam_ref[0, 0, p, rw, :] for (p, _), rw in zip(inst, rows)]
        kap = [kap_ref[0, p, rw, :].astype(F32) for (p, _), rw in zip(inst, rows)]
        v_z = [stack(v_ref[0, p, rw, :].astype(F32)).astype(BF16) for (p, _), rw in zip(inst, rows)]
        a = [a_ref[0, 0, p, rw, :].astype(F32) for (p, _), rw in zip(inst, rows)]
        kd = [k_ref[0, p, rw, :].astype(F32) * (1.0 + (ai - 1.0) * ka_ref[p])
              for (p, _), rw, ai in zip(inst, rows, a)]
        bet = each(lambda x, y: x * y, kap, a)

        def prefix(l):
            hi = l.astype(BF16)
            lo = (l - hi.astype(F32)).astype(BF16)
            cs = jnp.dot(cum_mask, jnp.concatenate([hi, lo], axis=1), preferred_element_type=F32)
            return cs[:, :128] + cs[:, 128:]

        lc = each(prefix, lam)
        ltot = [jnp.sum(l, axis=0, keepdims=True) for l in lam]
        e_neg = [jnp.exp(-x) for x in lc]
        qk_z = each(lambda kp, c, l: stack(kp * jnp.exp(c - l)).astype(BF16), kap, lc, lam)
        rt_z = [stack(r_ref[0, p, rw, :].astype(F32) * jnp.exp(c)) for (p, _), rw, c in zip(inst, rows, lc)]

        def scores(q, rz, kdi, bi, en):
            kt = (kdi * en).astype(BF16)
            bt = (bi * en).astype(BF16)
            return _dot_nt(jnp.concatenate([q, rz.astype(BF16)], axis=0), jnp.concatenate([kt, kt, bt, bt], axis=0))

        sc = each(scores, qk_z, rt_z, kd, bet, e_neg)
        m_k = [jnp.where(strict, s[:128, :128], 0.0).astype(BF16) for s in sc]
        a_kb = [jnp.concatenate([jnp.where(incl, s[128:, :128], 0.0), jnp.where(incl, -s[128:, 128:], 0.0)],
                                axis=1).astype(BF16) for s in sc]
        mkv = each(_dot, m_k, v_z)

        levels = C.bit_length() - 1
        pw = [jnp.where(strict, -s[:128, 128:], 0.0) for s in sc]
        t_inv = [jnp.where(eye, 1.0, 0.0) + x for x in pw]
        pw = [_dot(x, x) for x in pw]
        for _ in range(1, levels - 1):
            both = each(lambda x, t: _dot(x, jnp.concatenate([x, t], axis=1)), pw, t_inv)
            pw = [x[:, :128] for x in both]
            t_inv = each(lambda t, x: t + x[:, 128:], t_inv, both)
        t_inv = each(lambda t, x: t + _dot(x, t), t_inv, pw)

        tw = each(lambda t, q, m: _dot(t, jnp.concatenate([q, m.astype(BF16)], axis=1)).astype(BF16),
                  t_inv, qk_z, mkv)
        rhs = each(lambda vz, t: jnp.concatenate(
            [jnp.concatenate([vz, jnp.zeros_like(vz)], axis=1),
             jnp.concatenate([t[:, 128:], t[:, :128]], axis=1)], axis=0), v_z, tw)
        yr = each(_dot, a_kb, rhs)
        e_tail = each(lambda lt, c: jnp.exp(lt - c), ltot, lc)
        khb = each(lambda x, b, e: jnp.concatenate([stack(x * e), stack(-b * e)], axis=0), kd, bet, e_tail)
        hg = each(_dot_tn, khb, rhs)
        rhat = each(lambda rz, x: unstack(rz + x[:, 128:]), rt_z, yr)
        g = each(lambda lt, x: jnp.where(eye, jnp.exp(lt), 0.0) + x[:, 128:], ltot, hg)

        def advance(s0, gi, rh):
            s_hi = s0.astype(BF16)
            s_lo = (s0 - s_hi.astype(F32)).astype(BF16)
            gs = _dot(jnp.concatenate([gi, rh], axis=0), jnp.concatenate([s_hi, s_lo], axis=1))
            return gs[:, :128] + gs[:, 128:]

        state = [s_ref[p] for p in ps]
        n = len(ps)
        for q in range(WKV_SUB):
            sl = slice(q * n, (q + 1) * n)
            gs = each(advance, state, g[sl], rhat[sl])
            state = each(lambda x, h: x[:128] + h[:, :128], gs, hg[sl])
            for p, x, yl in zip(ps, gs, yr[sl]):
                y_ref[0, 0, p, pl.ds(offs[q], C), :] = (x[128:] + unstack(yl[:, :128])).astype(y_ref.dtype)
        for p, x in zip(ps, state):
            s_ref[p] = x

    for first in range(0, PAIRS, WKV_GROUP):
        group(range(first, first + WKV_GROUP))


def _wkv_call(k, kap, v, r, a, lam, ka, ctx_len, out_dtype=BF16):
    B, _, T, _ = k.shape
    C = WKV_C * WKV_SUB
    assert T % C == 0 and ctx_len % C == 0
    ns = T // C
    nc = ctx_len // C

    def chunk(d, j):
        rev = jnp.where(j < nc, nc - 1 - j, ns - 1 + nc - j)
        return jnp.where(d == 0, j, rev)

    tok = pl.BlockSpec((1, PAIRS, C, 128), lambda b, d, j: (b, 0, chunk(d, j), 0))
    per_dir = pl.BlockSpec((1, 1, PAIRS, C, 128), lambda b, d, j: (d, b, 0, chunk(d, j), 0))
    return pl.pallas_call(
        _wkv_kernel,
        out_shape=jax.ShapeDtypeStruct((2, B, PAIRS, T, 128), out_dtype),
        grid=(B, 2, ns),
        in_specs=[tok, tok, tok, tok, per_dir, per_dir, _full((PAIRS, 1, 128))],
        out_specs=per_dir,
        scratch_shapes=[pltpu.VMEM((PAIRS, 128, 128), F32)],
        compiler_params=_cparams(("parallel", "parallel", "arbitrary")),
        name="wkv_chunked",
    )(k, kap, v, r, a, lam, ka)


def _mixer_tail(x, mix_out, m, n2g, wrt, x_o, h_o, lg_o):
    x_new = x + m[2:3] * mix_out
    h2 = _adanorm(x_new, n2g[...], m[3:4], m[4:5])
    x_o[0] = x_new
    h_o[0] = h2.astype(h_o.dtype)
    lg_o[0] = lax.dot_general(wrt[...], h2, (((1,), (1,)), ((), ())), preferred_element_type=F32)


def _tail_out(B, nt):
    t_out = nt * TM
    shapes = [jax.ShapeDtypeStruct((B, t_out, D), F32), jax.ShapeDtypeStruct((B, t_out, D), BF16),
              jax.ShapeDtypeStruct((B, N_EXPERTS, t_out), F32)]
    tile = pl.BlockSpec((1, TM, D), lambda b, j: (b, j, 0))
    specs = [tile, tile, pl.BlockSpec((1, N_EXPERTS, TM), lambda b, j: (b, 0, j))]
    return shapes, specs


def _rwkv_post_kernel(y0, y1, bon, g, x, modc, lng, lnb, wo, n2g, wrt, ep, x_o, h_o, lg_o):
    parts = []
    for p in range(PAIRS):
        y = y0[0, 0, p].astype(F32) + y1[0, 0, p].astype(F32)
        dlt = y - _dot(y, ep[...]) * (1.0 / HEAD)
        var = _dot(dlt * dlt, ep[...]) * (1.0 / HEAD)
        parts.append(dlt * lax.rsqrt(var + GN_EPS))
    yn = jnp.concatenate(parts, axis=1)
    z = (yn * lng[...] + lnb[...] + bon[0].astype(F32)) * g[0].astype(F32)
    _mixer_tail(x[0], _dot(z, wo[...]), modc[0, 0], n2g, wrt, x_o, h_o, lg_o)


def _rwkv_post_call(y, bonus, g, x, mods, layer, prm, n2g, wrt, ctx_out):
    B, T, _ = x.shape
    nt = T // TM
    off = 0 if ctx_out else 1
    tile = pl.BlockSpec((1, TM, D), lambda b, j: (b, j + off, 0))
    ydir = [pl.BlockSpec((1, 1, PAIRS, TM, 128), functools.partial(lambda b, j, d: (d, b, 0, j + off, 0), d=d))
            for d in range(2)]
    consts = [prm["lng"], prm["lnb"], prm["wo"], n2g, wrt, prm["epair"]]
    shapes, ospecs = _tail_out(B, nt - off)
    return pl.pallas_call(
        _rwkv_post_kernel, out_shape=shapes, grid=(B, nt - off),
        in_specs=ydir + [tile, tile, tile, _mod_spec(layer, B, ctx_out)] + [_full(a.shape) for a in consts],
        out_specs=ospecs,
        compiler_params=_cparams(("parallel", "parallel")),
        name="rwkv_post",
    )(y, y, bonus, g, x, mods, *consts)


def _sgu_kernel(x, f, modp, modc, n1g, win, bin_, lng, lnb, ws, bs, wout, n2g, wrt, x_o, h_o, lg_o):
    x1 = x[0] + modp[0, 0, 5:6, :] * f[0]
    m = modc[0, 0]
    h = _adanorm(x1, n1g[...], m[0:1], m[1:2])
    z = _dot(h, win[...]) + bin_[...]
    z = 0.5 * z * (1.0 + lax.erf(z * 0.7071067811865476))
    u, v = z[:, :SG_HALF], z[:, SG_HALF:]
    mu = jnp.mean(v, axis=-1, keepdims=True)
    dv = v - mu
    var = jnp.mean(dv * dv, axis=-1, keepdims=True)
    vn = (dv * lax.rsqrt(var + LN_EPS) * lng[...] + lnb[...]).astype(BF16)
    gw = SG_HALF // SG_GROUPS
    rows = []
    for c in range(TM // SG_CHUNK):
        rs = slice(c * SG_CHUNK, (c + 1) * SG_CHUNK)
        cols = []
        for gi in range(SG_GROUPS):
            cs = slice(gi * gw, (gi + 1) * gw)
            vs = jnp.dot(ws[gi], vn[rs, cs], preferred_element_type=F32) + bs[gi]
            cols.append((u[rs, cs] * vs).astype(BF16))
        rows.append(jnp.concatenate(cols, axis=1))
    gated = jnp.concatenate(rows, axis=0)
    _mixer_tail(x1, _dot(gated, wout[...]), m, n2g, wrt, x_o, h_o, lg_o)


def _sgu_call(x, f, mods, layer, prm, n2g, wrt, has_ctx):
    B, T, _ = x.shape
    nt = T // TM
    tile = pl.BlockSpec((1, TM, D), lambda b, j: (b, j, 0))
    consts = [prm["n1g"], prm["win"], prm["bin"], prm["lng"], prm["lnb"], prm["ws"], prm["bs"], prm["wout"],
              n2g, wrt]
    shapes, ospecs = _tail_out(B, nt)
    return pl.pallas_call(
        _sgu_kernel, out_shape=shapes, grid=(B, nt),
        in_specs=[tile, tile, _mod_spec(layer - 1, B, has_ctx), _mod_spec(layer, B, has_ctx)]
        + [_full(a.shape) for a in consts],
        out_specs=ospecs,
        compiler_params=_cparams(("parallel", "parallel")),
        name="chunk_sgu",
    )(x, f, mods, mods, *consts)


def _route_kernel(lg_ref, rb_ref, slot_ref, wt_ref, cnt_ref, *, W):
    E, G = N_EXPERTS, N_GROUPS
    per = E // G
    s = _sigmoid(lg_ref[0])
    biased = s + rb_ref[:, 0:1]
    eidx = lax.broadcasted_iota(jnp.int32, (E, W), 0)

    best = None
    for gi in range(G):
        a, b, c, d = (biased[per * gi + i:per * gi + i + 1, :] for i in range(per))
        hi1, lo1, hi2, lo2 = jnp.maximum(a, b), jnp.minimum(a, b), jnp.maximum(c, d), jnp.minimum(c, d)
        top2 = jnp.maximum(hi1, hi2) + jnp.maximum(jnp.minimum(hi1, hi2), jnp.maximum(lo1, lo2))
        if best is None:
            best, bg = top2, jnp.zeros((1, W), jnp.int32)
        else:
            upd = top2 > best
            bg = jnp.where(upd, gi, bg)
            best = jnp.where(upd, top2, best)
    neg = -jnp.inf
    m1 = jnp.where((eidx // per) == bg, biased, neg)
    i1 = jnp.min(jnp.where(m1 == jnp.max(m1, axis=0, keepdims=True), eidx, E), axis=0, keepdims=True)
    sel1 = eidx == i1
    m2 = jnp.where(sel1, neg, m1)
    i2 = jnp.min(jnp.where(m2 == jnp.max(m2, axis=0, keepdims=True), eidx, E), axis=0, keepdims=True)
    sel2 = eidx == i2
    w1 = jnp.sum(jnp.where(sel1, s, 0.0), axis=0, keepdims=True)
    w2 = jnp.sum(jnp.where(sel2, s, 0.0), axis=0, keepdims=True)
    wt_ref[0] = jnp.concatenate([w1 / (w1 + w2), w2 / (w1 + w2)], axis=0)

    onehot = jnp.where(sel1 | sel2, 1.0, 0.0)
    blk = MOE_BLOCK
    upper = jnp.where(lax.broadcasted_iota(jnp.int32, (blk, blk), 0) < lax.broadcasted_iota(jnp.int32, (blk, blk), 1),
                      1.0, 0.0).astype(BF16)
    slots, sizes = [], []
    for i in range(W // blk):
        cols = slice(i * blk, (i + 1) * blk)
        ob = onehot[:, cols]
        rank = jnp.dot(ob.astype(BF16), upper, preferred_element_type=F32)
        size = jnp.ceil(jnp.sum(ob, axis=1, keepdims=True) * (1.0 / MOE_G)) * MOE_G
        offs, cum = [], jnp.zeros((1, 1), F32)
        for e in range(E):
            offs.append(cum)
            cum = cum + size[e:e + 1, :]
        pos = jnp.concatenate(offs, axis=0) + rank
        slots.append(jnp.concatenate([jnp.sum(jnp.where(sel1[:, cols], pos, 0.0), axis=0, keepdims=True),
                                      jnp.sum(jnp.where(sel2[:, cols], pos, 0.0), axis=0, keepdims=True)], axis=0))
        sizes.append(size)
    slot_ref[0] = jnp.concatenate(slots, axis=1).astype(jnp.int32)
    sizes.append(jnp.zeros((E, 128 - len(sizes)), F32))
    cnt_ref[0] = jnp.concatenate(sizes, axis=1).astype(jnp.int32)


def _route_call(logits, rbias):
    B, _, W = logits.shape
    win = lambda w: (w, 0, 0)
    return pl.pallas_call(
        functools.partial(_route_kernel, W=W),
        out_shape=[jax.ShapeDtypeStruct((B, 2, W), jnp.int32), jax.ShapeDtypeStruct((B, 2, W), F32),
                   jax.ShapeDtypeStruct((B, N_EXPERTS, 128), jnp.int32)],
        grid=(B,),
        in_specs=[pl.BlockSpec((1, N_EXPERTS, W), win), _full(rbias.shape)],
        out_specs=[pl.BlockSpec((1, 2, W), win), pl.BlockSpec((1, 2, W), win),
                   pl.BlockSpec((1, N_EXPERTS, 128), win)],
        compiler_params=_cparams(("parallel",)),
        name="moe_route",
    )(logits, rbias)


def _dispatch_tables(sizes, nb):
    max_rows = nb * MOE_CAP
    max_tiles = -(-max_rows // MOE_TM) + N_EXPERTS
    rows = sizes.sum(-1)
    rows_p = -(-rows // MOE_TM) * MOE_TM
    end = jnp.cumsum(rows_p, axis=1)
    start = end - rows_p
    n_tiles = end[:, -1] // MOE_TM
    last = jnp.max(jnp.where(rows > 0, jnp.arange(N_EXPERTS)[None, :], 0), axis=1, keepdims=True)

    def owner(first_row):
        return jnp.minimum(jnp.sum(end[:, :, None] <= first_row[None, None, :], axis=1), last)

    texp = owner(jnp.arange(max_tiles) * MOE_TM)
    q_row = jnp.arange(max_tiles * MOE_TM // MOE_G) * MOE_G
    e_q = owner(q_row)
    off = q_row[None, :] - jnp.take_along_axis(start, e_q, axis=1)
    valid = off < jnp.take_along_axis(rows, e_q, axis=1)
    seg_end = jnp.cumsum(sizes, axis=2)
    seg_end_q = jnp.take_along_axis(seg_end, e_q[:, :, None], axis=1)
    b_q = jnp.minimum(jnp.sum(seg_end_q <= off[:, :, None], axis=2), nb - 1)
    local = jnp.cumsum(sizes, axis=1) - sizes

    def pick(a):
        return jnp.take_along_axis(jnp.take_along_axis(a, e_q[:, :, None], axis=1), b_q[:, :, None], axis=2)[..., 0]

    src_row = b_q * MOE_CAP + pick(local) + off - (pick(seg_end) - pick(sizes))
    table = jnp.where(valid, src_row // MOE_G, nb * MOE_CAP // MOE_G).astype(jnp.int32)
    return texp.astype(jnp.int32).reshape(-1), n_tiles.astype(jnp.int32), table[:, None, :], max_tiles


def _moe_kernel(texp_ref, nt_ref, h_ref, slot_ref, col_ref, tab_ref, wgu_ref, wd_ref, o_ref, rows_ref, xt_ref,
                *, nb):
    w = pl.program_id(0)
    t = pl.program_id(1)
    cap, gpt = MOE_CAP, MOE_TM // MOE_G
    zero_group = nb * cap // MOE_G

    @pl.when(t == 0)
    def _():
        rows_ref[nb * cap:nb * cap + 2 * MOE_G, :] = jnp.zeros((2 * MOE_G, D), rows_ref.dtype)
        rid = lax.broadcasted_iota(jnp.int32, (cap, MOE_BLOCK), 0)
        for b in range(nb):
            cols = slice(b * MOE_BLOCK, (b + 1) * MOE_BLOCK)
            hit = (slot_ref[0, 0:1, cols] == rid) | (slot_ref[0, 1:2, cols] == rid)
            disp = jnp.where(hit, 1.0, 0.0).astype(BF16)
            rows_ref[b * cap:(b + 1) * cap, :] = jnp.dot(
                disp, h_ref[0, cols, :], preferred_element_type=F32).astype(rows_ref.dtype)

    @pl.when(t < nt_ref[w])
    def _():
        src = [tab_ref[0, 0, t * gpt + g] for g in range(gpt)]
        for g in range(gpt):
            xt_ref[g * MOE_G:(g + 1) * MOE_G, :] = rows_ref[pl.ds(pl.multiple_of(src[g] * MOE_G, MOE_G), MOE_G), :]
        gu = jnp.dot(xt_ref[...], wgu_ref[0], preferred_element_type=F32)
        gate, up = gu[:, :EXPERT_FFN], gu[:, EXPERT_FFN:]
        y = _dot(gate * _sigmoid(gate) * up, wd_ref[0]).astype(rows_ref.dtype)
        for g in range(gpt):
            dst = jnp.where(src[g] == zero_group, zero_group + 1, src[g])
            rows_ref[pl.ds(pl.multiple_of(dst * MOE_G, MOE_G), MOE_G), :] = y[g * MOE_G:(g + 1) * MOE_G, :]

    @pl.when(t == pl.num_programs(1) - 1)
    def _():
        cid = lax.broadcasted_iota(jnp.int32, (MOE_BLOCK, cap), 1).astype(F32)
        for b in range(nb):
            col = col_ref[0, b * MOE_BLOCK:(b + 1) * MOE_BLOCK, :]
            comb = (jnp.where(col[:, 0:1] == cid, col[:, 2:3], 0.0)
                    + jnp.where(col[:, 1:2] == cid, col[:, 3:4], 0.0)).astype(BF16)
            o_ref[0, b * MOE_BLOCK:(b + 1) * MOE_BLOCK, :] = jnp.dot(
                comb, rows_ref[b * cap:(b + 1) * cap, :], preferred_element_type=F32).astype(o_ref.dtype)


def _moe_call(h2, slots, colinfo, texp, ntiles, table, max_tiles, wgu, wd):
    nw, W, _ = h2.shape
    nb = W // MOE_BLOCK
    win = lambda w, t, te, nt: (w, 0, 0)
    expert = lambda w, t, te, nt: (te[w * max_tiles + t], 0, 0)
    grid_spec = pltpu.PrefetchScalarGridSpec(
        num_scalar_prefetch=2, grid=(nw, max_tiles),
        in_specs=[pl.BlockSpec((1, W, D), win), pl.BlockSpec((1, 2, W), win), pl.BlockSpec((1, W, 4), win),
                  pl.BlockSpec((1, 1, table.shape[-1]), win, memory_space=pltpu.SMEM),
                  pl.BlockSpec((1, D, 2 * EXPERT_FFN), expert), pl.BlockSpec((1, EXPERT_FFN, D), expert)],
        out_specs=pl.BlockSpec((1, W, D), win),
        scratch_shapes=[pltpu.VMEM((nb * MOE_CAP + 2 * MOE_G, D), BF16), pltpu.VMEM((MOE_TM, D), BF16)])
    return pl.pallas_call(
        functools.partial(_moe_kernel, nb=nb),
        out_shape=jax.ShapeDtypeStruct((nw, W, D), BF16),
        grid_spec=grid_spec,
        compiler_params=_cparams(("parallel", "arbitrary")),
        name="moe_experts",
    )(texp, ntiles, h2, slots, colinfo, table, wgu, wd)


def _moe_layer(h2, logits, rbias, wgu, wd):
    nb = h2.shape[1] // MOE_BLOCK
    slots, wts, sizes = _route_call(logits, rbias)
    texp, ntiles, table, max_tiles = _dispatch_tables(sizes[:, :, :nb], nb)
    colinfo = jnp.concatenate([slots.astype(F32), wts], axis=1).transpose(0, 2, 1)
    return _moe_call(h2, slots, colinfo, texp, ntiles, table, max_tiles, wgu, wd)


def _final_kernel(x, f, modp, g, o):
    o[0] = _rmsnorm(x[0] + modp[0, 0, 5:6, :] * f[0]) * g[...]


def _final_call(x, f, mods, layer, final_g):
    B, T, _ = x.shape
    tile = pl.BlockSpec((1, TM, D), lambda b, j: (b, j, 0))
    return pl.pallas_call(
        _final_kernel, out_shape=jax.ShapeDtypeStruct((B, T, D), F32), grid=(B, T // TM),
        in_specs=[tile, tile, _mod_spec(layer, B, False), _full(final_g.shape)], out_specs=tile,
        compiler_params=_cparams(("parallel", "parallel")),
        name="final_norm",
    )(x, f, mods, final_g)


def _rwkv_params(a, norm1_g, layer, rw_mu, rw_wr, rw_wk, rw_wv, rw_wo, rw_w0, rw_w1, rw_w2, rw_a0, rw_a1, rw_a2,
                 rw_g1, rw_g2, rw_kk, rw_ka, rw_rk, rw_lnx_g, rw_lnx_b, epair):
    lora = rw_w1.shape[-1]

    def block_diag(w2):
        z = jnp.zeros((lora, D), w2.dtype)
        return jnp.concatenate([jnp.concatenate([w2[0], z], axis=1), jnp.concatenate([z, w2[1]], axis=1)], axis=0)

    return dict(
        n1g=norm1_g[layer].reshape(1, D), mu=rw_mu[a],
        wr=rw_wr[a].astype(BF16), wk=rw_wk[a].astype(BF16), wv=rw_wv[a].astype(BF16), wo=rw_wo[a].astype(BF16),
        w1c=jnp.concatenate([rw_w1[a, 0], rw_w1[a, 1]], axis=1).astype(BF16), w2b=block_diag(rw_w2[a]).astype(BF16),
        w0=rw_w0[a],
        a1c=jnp.concatenate([rw_a1[a, 0], rw_a1[a, 1]], axis=1).astype(BF16), a2b=block_diag(rw_a2[a]).astype(BF16),
        a0=rw_a0[a],
        g1=rw_g1[a].astype(BF16), g2=rw_g2[a].astype(BF16),
        kk=rw_kk[a].reshape(1, D), ka=rw_ka[a].reshape(1, D), rk=rw_rk[a].reshape(1, D),
        lng=rw_lnx_g[a].reshape(1, D), lnb=rw_lnx_b[a].reshape(1, D), epair=epair)


def kernel(x, c, ctx, c_ctx, w_mod, b_mod, norm1_g, norm2_g, rw_mu, rw_wr, rw_wk, rw_wv, rw_wo, rw_w0, rw_w1, rw_w2, rw_a0, rw_a1, rw_a2, rw_v0, rw_v1, rw_v2, rw_g1, rw_g2, rw_kk, rw_ka, rw_rk, rw_lnx_g, rw_lnx_b, sg_w_in, sg_b_in, sg_ln_g, sg_ln_b, sg_w_s, sg_b_s, sg_w_out, w_router, router_bias, ex_w_gate, ex_w_up, ex_w_down, final_g):
    B, L, _ = x.shape
    Lc = ctx.shape[1]
    assert Lc == TM and L % TM == 0 and B + 1 <= MOD_ROWS and w_mod.shape[0] == 4

    cvec = jnp.concatenate([c, c_ctx[None, :], jnp.zeros((MOD_ROWS - B - 1, D), F32)], axis=0)
    mods = _mod_call(cvec, w_mod, b_mod).reshape(w_mod.shape[0], MOD_ROWS, N_MOD, D)
    xs = jnp.concatenate([ctx, x], axis=1)

    lane = jnp.arange(128) // HEAD
    epair = (lane[:, None] == lane[None, :]).astype(BF16)
    wrt = w_router.T
    rbias = jnp.broadcast_to(router_bias[:, None], (N_EXPERTS, 128))
    n2g = [norm2_g[i].reshape(1, D) for i in range(4)]
    wgu = [jnp.concatenate([ex_w_gate[i], ex_w_up[i]], axis=-1).astype(BF16) for i in range(4)]
    wd = [ex_w_down[i].astype(BF16) for i in range(4)]
    rw_common = (rw_mu, rw_wr, rw_wk, rw_wv, rw_wo, rw_w0, rw_w1, rw_w2, rw_a0, rw_a1, rw_a2, rw_g1, rw_g2,
                 rw_kk, rw_ka, rw_rk, rw_lnx_g, rw_lnx_b, epair)

    def sgu_params(b, layer):
        return dict(n1g=norm1_g[layer].reshape(1, D), win=sg_w_in[b].astype(BF16), bin=sg_b_in[b].reshape(1, -1),
                    lng=sg_ln_g[b].reshape(1, -1), lnb=sg_ln_b[b].reshape(1, -1), ws=sg_w_s[b].astype(BF16),
                    bs=jnp.broadcast_to(sg_b_s[b][:, :, None], (SG_GROUPS, SG_CHUNK, SG_HALF // SG_GROUPS)),
                    wout=sg_w_out[b].astype(BF16))

    p0 = _rwkv_params(0, norm1_g, 0, *rw_common)
    k0, kap0, v0, r0, a0, lam0, g0, bon0 = _rwkv_pre_call(xs, None, mods, 0, p0, None, None)
    y0 = _wkv_call(k0, kap0, v0, r0, a0, lam0, p0["ka"].reshape(PAIRS, 1, 128), Lc)
    x1, h1, lg1 = _rwkv_post_call(y0, bon0, g0, xs, mods, 0, p0, n2g[0], wrt, True)
    f0 = _moe_layer(h1, lg1, rbias, wgu[0], wd[0])

    x2, h2, lg2 = _sgu_call(x1, f0, mods, 1, sgu_params(0, 1), n2g[1], wrt, True)
    f1 = _moe_layer(h2, lg2, rbias, wgu[1], wd[1])

    p2 = _rwkv_params(1, norm1_g, 2, *rw_common)
    vres = dict(v0=rw_v0[0].reshape(1, D),
                v1=jnp.pad(rw_v1[0], ((0, 0), (0, 128 - rw_v1.shape[-1]))).astype(BF16),
                v2=jnp.pad(rw_v2[0], ((0, 128 - rw_v2.shape[-2]), (0, 0))).astype(BF16))
    k2, kap2, v2, r2, a2, lam2, g2, bon2, x2r = _rwkv_pre_call(x2, f1, mods, 2, p2, vres, v0)
    y2 = _wkv_call(k2, kap2, v2, r2, a2, lam2, p2["ka"].reshape(PAIRS, 1, 128), Lc)
    x3, h3, lg3 = _rwkv_post_call(y2, bon2, g2, x2r, mods, 2, p2, n2g[2], wrt, False)
    f2 = _moe_layer(h3, lg3, rbias, wgu[2], wd[2])

    x4, h4, lg4 = _sgu_call(x3, f2, mods, 3, sgu_params(1, 3), n2g[3], wrt, False)
    f3 = _moe_layer(h4, lg4, rbias, wgu[3], wd[3])
    return _final_call(x4, f3, mods, 3, final_g.reshape(1, D))
```

```python
import functools

import jax
import jax.numpy as jnp
from jax import lax
from jax.experimental import pallas as pl
from jax.experimental.pallas import tpu as pltpu

F32 = jnp.float32
BF16 = jnp.bfloat16

D = 1024
HEAD = 64
PAIRS = D // 128
GRID_W = 64
N_MOD = 6
N_EXPERTS = 16
N_GROUPS = 4
EXPERT_FFN = 512
SG_CHUNK = 128
SG_GROUPS = 16
SG_HALF = 2048
NORM_EPS = 1e-6
LN_EPS = 1e-5
GN_EPS = 64e-5
EXP_NEG_HALF = 0.6065306597126334

TM = 256
HALO = GRID_W
WKV_C = 64
WKV_GROUP = 8
WKV_SUB = 4
MOE_TM = 256
MOE_TPS = 2
MOE_BLOCK = 256
MOE_G = 16
MOE_CAP = 2 * MOE_BLOCK + N_EXPERTS * (MOE_G - 1)
MOD_ROWS = 40
VMEM_LIMIT = 56 * 1024 * 1024


def _cparams(sem):
    return pltpu.CompilerParams(dimension_semantics=sem, vmem_limit_bytes=VMEM_LIMIT)


def _dot(a, b):
    return jnp.dot(a.astype(BF16), b.astype(BF16), preferred_element_type=F32)


def _dot_nt(a, b):
    return lax.dot_general(a.astype(BF16), b.astype(BF16), (((1,), (1,)), ((), ())),
                           preferred_element_type=F32)


def _dot_tn(a, b):
    return lax.dot_general(a.astype(BF16), b.astype(BF16), (((0,), (0,)), ((), ())),
                           preferred_element_type=F32)


def _sigmoid(x):
    return 1.0 / (1.0 + jnp.exp(-x))


def _rmsnorm(x):
    return x * lax.rsqrt(jnp.mean(x * x, axis=-1, keepdims=True) + NORM_EPS)


def _adanorm(x, g, shift, scale):
    return _rmsnorm(x) * g * (1.0 + scale) + shift


def _full(shape):
    zeros = (0,) * len(shape)
    return pl.BlockSpec(shape, lambda *_: zeros)


def _mod_kernel(c_ref, w_ref, b_ref, o_ref):
    c = c_ref[...]
    o_ref[0] = jnp.dot(c * _sigmoid(c), w_ref[0], preferred_element_type=F32) + b_ref[0]


def _mod_call(cvec, w_mod, b_mod):
    depth = w_mod.shape[0]
    return pl.pallas_call(
        _mod_kernel,
        out_shape=jax.ShapeDtypeStruct((depth, MOD_ROWS, N_MOD * D), F32),
        grid=(depth, N_MOD),
        in_specs=[pl.BlockSpec((MOD_ROWS, D), lambda i, n: (0, 0)),
                  pl.BlockSpec((1, D, D), lambda i, n: (i, 0, n)),
                  pl.BlockSpec((1, 1, D), lambda i, n: (i, 0, n))],
        out_specs=pl.BlockSpec((1, MOD_ROWS, D), lambda i, n: (i, 0, n)),
        compiler_params=_cparams(("parallel", "parallel")),
        name="modulation",
    )(cvec, w_mod, b_mod.reshape(depth, 1, N_MOD * D))


def _mod_spec(layer, batch, ctx_tile):
    if ctx_tile:
        return pl.BlockSpec((1, 1, N_MOD, D), lambda b, j: (layer, jnp.where(j == 0, batch, b), 0, 0))
    return pl.BlockSpec((1, 1, N_MOD, D), lambda b, j: (layer, b, 0, 0))


def _rwkv_pre_kernel(*refs, has_f, has_vres, n_tiles):
    it = iter(refs)
    xm, xp, xn = next(it), next(it), next(it)
    if has_f:
        fm, fp, fn, modp = next(it), next(it), next(it), next(it)
    modc, n1g, mu = next(it), next(it), next(it)
    wr, wk, wv = next(it), next(it), next(it)
    w1c, w2b, w0 = next(it), next(it), next(it)
    a1c, a2b, a0 = next(it), next(it), next(it)
    g1, g2 = next(it), next(it)
    if has_vres:
        v0, v1, v2, vf = next(it), next(it), next(it), next(it)
    kkv, kav, rkv, ep = next(it), next(it), next(it), next(it)
    k_o, kap_o, v_o, r_o, a_o, lam_o, g_o, bon_o = (next(it) for _ in range(8))
    if has_f:
        x_o = next(it)

    j = pl.program_id(1)
    x_all = jnp.concatenate([xp[0], xm[0], xn[0]], axis=0)
    if has_f:
        f_all = jnp.concatenate([fp[0], fm[0], fn[0]], axis=0)
        x_all = x_all + modp[0, 0, 5:6, :] * f_all
        x_o[0] = x_all[HALO:HALO + TM]
    m = modc[0, 0]
    h_all = _adanorm(x_all, n1g[...], m[0:1], m[1:2])
    hm = h_all[HALO:HALO + TM]

    q = D // 4
    r = lax.broadcasted_iota(jnp.int32, (TM, q), 0)
    jv = jnp.zeros((TM, q), jnp.int32) + j
    is_ctx = jv == 0
    colp = r % GRID_W
    prev = pltpu.roll(hm, 1, axis=0)
    nxt = pltpu.roll(hm, TM - 1, axis=0)
    up = h_all[0:TM]
    down = h_all[2 * HALO:2 * HALO + TM]
    first, last = r == 0, r == TM - 1
    q0 = jnp.where((is_ctx & first) | (~is_ctx & (colp == 0)), 0.0, prev[:, 0:q])
    q1 = jnp.where(is_ctx, jnp.where(first, 0.0, prev[:, q:2 * q]),
                   jnp.where(colp == GRID_W - 1, 0.0, nxt[:, q:2 * q]))
    q2 = jnp.where(is_ctx, jnp.where(last, 0.0, nxt[:, 2 * q:3 * q]),
                   jnp.where((jv == 1) & (r < HALO), 0.0, up[:, 2 * q:3 * q]))
    q3 = jnp.where(is_ctx, jnp.where(last, 0.0, nxt[:, 3 * q:]),
                   jnp.where((jv == n_tiles - 1) & (r >= TM - HALO), 0.0, down[:, 3 * q:]))
    xx = jnp.concatenate([q0, q1, q2, q3], axis=1) - hm

    hm_b, xx_b, mu_b = hm.astype(BF16), xx.astype(BF16), mu[...].astype(BF16)

    def mix(i):
        return hm_b + xx_b * mu_b[i:i + 1, :]

    xr, xw, xk, xv, xa, xg = (mix(i) for i in range(6))
    rr = _dot(xr, wr[...])
    kk = _dot(xk, wk[...])
    vv = _dot(xv, wv[...])
    if has_vres:
        gate = _sigmoid(v0[...] + _dot(_dot(xv, v1[...]), v2[...]))
        vfirst = jnp.concatenate([vf[0, p].astype(F32) for p in range(PAIRS)], axis=1)
        vv = vv + (vfirst - vv) * gate
    g_o[0] = _dot(_sigmoid(_dot(xg, g1[...])), g2[...]).astype(g_o.dtype)

    zw = _dot(jnp.tanh(_dot(xw, w1c[...])), w2b[...])
    za = _dot(_dot(xa, a1c[...]), a2b[...])
    ka = kav[...]
    kd_sum = jnp.zeros((TM, D), F32)
    for d in range(2):
        z = w0[d:d + 1, :] + zw[:, d * D:(d + 1) * D]
        lam = -EXP_NEG_HALF * _sigmoid(z)
        a = _sigmoid(a0[d:d + 1, :] + za[:, d * D:(d + 1) * D])
        kd_sum = kd_sum + kk * (1.0 + (a - 1.0) * ka)
        for p in range(PAIRS):
            sl = slice(128 * p, 128 * (p + 1))
            lam_o[d, 0, p] = lam[:, sl]
            a_o[d, 0, p] = a[:, sl].astype(a_o.dtype)

    kkk = kk * kkv[...]
    rk = rr * rkv[...] * kd_sum
    for p in range(PAIRS):
        sl = slice(128 * p, 128 * (p + 1))
        kp = kkk[:, sl]
        ss = _dot(kp * kp, ep[...])
        kap_o[0, p] = (kp * lax.rsqrt(jnp.maximum(ss, 1e-24))).astype(kap_o.dtype)
        bon_o[0, :, sl] = (_dot(rk[:, sl], ep[...]) * vv[:, sl]).astype(bon_o.dtype)
        k_o[0, p] = kk[:, sl].astype(k_o.dtype)
        v_o[0, p] = vv[:, sl].astype(v_o.dtype)
        r_o[0, p] = rr[:, sl].astype(r_o.dtype)


def _rwkv_pre_call(xs, f_prev, mods, layer, prm, vres, v_first):
    B, T, _ = xs.shape
    nt = T // TM
    nh = T // HALO
    has_f = f_prev is not None
    has_vres = vres is not None
    main = pl.BlockSpec((1, TM, D), lambda b, j: (b, j, 0))
    prev = pl.BlockSpec((1, HALO, D), lambda b, j: (b, jnp.maximum(j * (TM // HALO) - 1, 0), 0))
    nxt = pl.BlockSpec((1, HALO, D), lambda b, j: (b, jnp.minimum((j + 1) * (TM // HALO), nh - 1), 0))
    pm = pl.BlockSpec((1, PAIRS, TM, 128), lambda b, j: (b, 0, j, 0))
    pm2 = pl.BlockSpec((2, 1, PAIRS, TM, 128), lambda b, j: (0, b, 0, j, 0))

    args, specs = [xs, xs, xs], [main, prev, nxt]
    if has_f:
        args += [f_prev, f_prev, f_prev, mods]
        specs += [main, prev, nxt, _mod_spec(layer - 1, B, True)]
    args += [mods, prm["n1g"], prm["mu"], prm["wr"], prm["wk"], prm["wv"], prm["w1c"], prm["w2b"], prm["w0"],
             prm["a1c"], prm["a2b"], prm["a0"], prm["g1"], prm["g2"]]
    specs += [_mod_spec(layer, B, True)] + [_full(a.shape) for a in args[len(specs) + 1:]]
    if has_vres:
        args += [vres["v0"], vres["v1"], vres["v2"], v_first]
        specs += [_full(vres["v0"].shape), _full(vres["v1"].shape), _full(vres["v2"].shape), pm]
    tail = [prm["kk"], prm["ka"], prm["rk"], prm["epair"]]
    args += tail
    specs += [_full(a.shape) for a in tail]

    pm_shape = jax.ShapeDtypeStruct((B, PAIRS, T, 128), BF16)
    out_shape = [pm_shape, pm_shape, pm_shape, pm_shape,
                 jax.ShapeDtypeStruct((2, B, PAIRS, T, 128), BF16),
                 jax.ShapeDtypeStruct((2, B, PAIRS, T, 128), F32),
                 jax.ShapeDtypeStruct((B, T, D), BF16),
                 jax.ShapeDtypeStruct((B, T, D), BF16)]
    out_specs = [pm, pm, pm, pm, pm2, pm2, main, main]
    if has_f:
        out_shape.append(jax.ShapeDtypeStruct((B, T, D), F32))
        out_specs.append(main)
    return pl.pallas_call(
        functools.partial(_rwkv_pre_kernel, has_f=has_f, has_vres=has_vres, n_tiles=nt),
        out_shape=out_shape, grid=(B, nt), in_specs=specs, out_specs=out_specs,
        compiler_params=_cparams(("parallel", "parallel")),
        name="rwkv_pre",
    )(*args)


def _wkv_kernel(k_ref, kap_ref, v_ref, r_ref, a_ref, lam_ref, ka_ref, y_ref, s_ref):
    C = WKV_C
    d = pl.program_id(1)
    j = pl.program_id(2)

    @pl.when(j == 0)
    def _():
        s_ref[...] = jnp.zeros_like(s_ref)

    dv = jnp.zeros((128, 128), jnp.int32) + d
    row = lax.broadcasted_iota(jnp.int32, (128, 128), 0)
    col = lax.broadcasted_iota(jnp.int32, (128, 128), 1)
    same = (row // C) == (col // C)
    tr, tc = row % C, col % C
    before = ((dv == 0) & (tc < tr)) | ((dv != 0) & (tc > tr))
    strict = same & before
    incl = same & (before | (tc == tr))
    eye = row == col
    cum_mask = jnp.where(incl[:C, :C], 1.0, 0.0).astype(BF16)
    lane_lo = lax.broadcasted_iota(jnp.int32, (C, 128), 1) < HEAD

    def stack(x):
        return jnp.concatenate([jnp.where(lane_lo, x, 0.0), jnp.where(lane_lo, 0.0, x)], axis=0)

    def unstack(z):
        return z[:C] + z[C:]

    offs = [pl.multiple_of(jnp.where(d == 0, q * C, (WKV_SUB - 1 - q) * C), C) for q in range(WKV_SUB)]

    def group(ps):
        def each(fn, *lists):
            return [fn(*xs) for xs in zip(*lists)]

        inst = [(p, q) for q in range(WKV_SUB) for p in ps]
        rows = [pl.ds(offs[q], C) for _, q in inst]
        lam = [lam_ref[0, 0, p, rw, :] for (p, _), rw in zip(inst, rows)]
        kap = [kap_ref[0, p, rw, :].astype(F32) for (p, _), rw in zip(inst, rows)]
        v_z = [stack(v_ref[0, p, rw, :].astype(F32)).astype(BF16) for (p, _), rw in zip(inst, rows)]
        a = [a_ref[0, 0, p, rw, :].astype(F32) for (p, _), rw in zip(inst, rows)]
        kd = [k_ref[0, p, rw, :].astype(F32) * (1.0 + (ai - 1.0) * ka_ref[p])
              for (p, _), rw, ai in zip(inst, rows, a)]
        bet = each(lambda x, y: x * y, kap, a)

        def prefix(l):
            hi = l.astype(BF16)
            lo = (l - hi.astype(F32)).astype(BF16)
            cs = jnp.dot(cum_mask, jnp.concatenate([hi, lo], axis=1), preferred_element_type=F32)
            return cs[:, :128] + cs[:, 128:]

        lc = each(prefix, lam)
        ltot = [jnp.sum(l, axis=0, keepdims=True) for l in lam]
        e_neg = [jnp.exp(-x) for x in lc]
        qk_z = each(lambda kp, c, l: stack(kp * jnp.exp(c - l)).astype(BF16), kap, lc, lam)
        rt_z = [stack(r_ref[0, p, rw, :].astype(F32) * jnp.exp(c)) for (p, _), rw, c in zip(inst, rows, lc)]

        def scores(q, rz, kdi, bi, en):
            kt = (kdi * en).astype(BF16)
            bt = (bi * en).astype(BF16)
            return _dot_nt(jnp.concatenate([q, rz.astype(BF16)], axis=0), jnp.concatenate([kt, kt, bt, bt], axis=0))

        sc = each(scores, qk_z, rt_z, kd, bet, e_neg)
        m_k = [jnp.where(strict, s[:128, :128], 0.0).astype(BF16) for s in sc]
        a_kb = [jnp.concatenate([jnp.where(incl, s[128:, :128], 0.0), jnp.where(incl, -s[128:, 128:], 0.0)],
                                axis=1).astype(BF16) for s in sc]
        mkv = each(_dot, m_k, v_z)

        levels = C.bit_length() - 1
        pw = [jnp.where(strict, -s[:128, 128:], 0.0) for s in sc]
        t_inv = [jnp.where(eye, 1.0, 0.0) + x for x in pw]
        pw = [_dot(x, x) for x in pw]
        for _ in range(1, levels - 1):
            both = each(lambda x, t: _dot(x, jnp.concatenate([x, t], axis=1)), pw, t_inv)
            pw = [x[:, :128] for x in both]
            t_inv = each(lambda t, x: t + x[:, 128:], t_inv, both)
        t_inv = each(lambda t, x: t + _dot(x, t), t_inv, pw)

        tw = each(lambda t, q, m: _dot(t, jnp.concatenate([q, m.astype(BF16)], axis=1)).astype(BF16),
                  t_inv, qk_z, mkv)
        rhs = each(lambda vz, t: jnp.concatenate(
            [jnp.concatenate([vz, jnp.zeros_like(vz)], axis=1),
             jnp.concatenate([t[:, 128:], t[:, :128]], axis=1)], axis=0), v_z, tw)
        yr = each(_dot, a_kb, rhs)
        e_tail = each(lambda lt, c: jnp.exp(lt - c), ltot, lc)
        khb = each(lambda x, b, e: jnp.concatenate([stack(x * e), stack(-b * e)], axis=0), kd, bet, e_tail)
        hg = each(_dot_tn, khb, rhs)
        rhat = each(lambda rz, x: unstack(rz + x[:, 128:]), rt_z, yr)
        g = each(lambda lt, x: jnp.where(eye, jnp.exp(lt), 0.0) + x[:, 128:], ltot, hg)

        def advance(s0, gi, rh):
            s_hi = s0.astype(BF16)
            s_lo = (s0 - s_hi.astype(F32)).astype(BF16)
            gs = _dot(jnp.concatenate([gi, rh], axis=0), jnp.concatenate([s_hi, s_lo], axis=1))
            return gs[:, :128] + gs[:, 128:]

        state = [s_ref[p] for p in ps]
        n = len(ps)
        for q in range(WKV_SUB):
            sl = slice(q * n, (q + 1) * n)
            gs = each(advance, state, g[sl], rhat[sl])
            state = each(lambda x, h: x[:128] + h[:, :128], gs, hg[sl])
            for p, x, yl in zip(ps, gs, yr[sl]):
                y_ref[0, 0, p, pl.ds(offs[q], C), :] = (x[128:] + unstack(yl[:, :128])).astype(y_ref.dtype)
        for p, x in zip(ps, state):
            s_ref[p] = x

    for first in range(0, PAIRS, WKV_GROUP):
        group(range(first, first + WKV_GROUP))


def _wkv_call(k, kap, v, r, a, lam, ka, ctx_len, out_dtype=BF16):
    B, _, T, _ = k.shape
    C = WKV_C * WKV_SUB
    assert T % C == 0 and ctx_len % C == 0
    ns = T // C
    nc = ctx_len // C

    def chunk(d, j):
        rev = jnp.where(j < nc, nc - 1 - j, ns - 1 + nc - j)
        return jnp.where(d == 0, j, rev)

    tok = pl.BlockSpec((1, PAIRS, C, 128), lambda b, d, j: (b, 0, chunk(d, j), 0))
    per_dir = pl.BlockSpec((1, 1, PAIRS, C, 128), lambda b, d, j: (d, b, 0, chunk(d, j), 0))
    return pl.pallas_call(
        _wkv_kernel,
        out_shape=jax.ShapeDtypeStruct((2, B, PAIRS, T, 128), out_dtype),
        grid=(B, 2, ns),
        in_specs=[tok, tok, tok, tok, per_dir, per_dir, _full((PAIRS, 1, 128))],
        out_specs=per_dir,
        scratch_shapes=[pltpu.VMEM((PAIRS, 128, 128), F32)],
        compiler_params=_cparams(("parallel", "parallel", "arbitrary")),
        name="wkv_chunked",
    )(k, kap, v, r, a, lam, ka)


def _mixer_tail(x, mix_out, m, n2g, wrt, x_o, h_o, lg_o):
    x_new = x + m[2:3] * mix_out
    h2 = _adanorm(x_new, n2g[...], m[3:4], m[4:5])
    x_o[0] = x_new
    h_o[0] = h2.astype(h_o.dtype)
    lg_o[0] = lax.dot_general(wrt[...], h2, (((1,), (1,)), ((), ())), preferred_element_type=F32)


def _tail_out(B, nt):
    t_out = nt * TM
    shapes = [jax.ShapeDtypeStruct((B, t_out, D), F32), jax.ShapeDtypeStruct((B, t_out, D), BF16),
              jax.ShapeDtypeStruct((B, N_EXPERTS, t_out), F32)]
    tile = pl.BlockSpec((1, TM, D), lambda b, j: (b, j, 0))
    specs = [tile, tile, pl.BlockSpec((1, N_EXPERTS, TM), lambda b, j: (b, 0, j))]
    return shapes, specs


def _rwkv_post_kernel(y0, y1, bon, g, x, modc, lng, lnb, wo, n2g, wrt, ep, x_o, h_o, lg_o):
    parts = []
    for p in range(PAIRS):
        y = y0[0, 0, p].astype(F32) + y1[0, 0, p].astype(F32)
        dlt = y - _dot(y, ep[...]) * (1.0 / HEAD)
        var = _dot(dlt * dlt, ep[...]) * (1.0 / HEAD)
        parts.append(dlt * lax.rsqrt(var + GN_EPS))
    yn = jnp.concatenate(parts, axis=1)
    z = (yn * lng[...] + lnb[...] + bon[0].astype(F32)) * g[0].astype(F32)
    _mixer_tail(x[0], _dot(z, wo[...]), modc[0, 0], n2g, wrt, x_o, h_o, lg_o)


def _rwkv_post_call(y, bonus, g, x, mods, layer, prm, n2g, wrt, ctx_out):
    B, T, _ = x.shape
    nt = T // TM
    off = 0 if ctx_out else 1
    tile = pl.BlockSpec((1, TM, D), lambda b, j: (b, j + off, 0))
    ydir = [pl.BlockSpec((1, 1, PAIRS, TM, 128), functools.partial(lambda b, j, d: (d, b, 0, j + off, 0), d=d))
            for d in range(2)]
    consts = [prm["lng"], prm["lnb"], prm["wo"], n2g, wrt, prm["epair"]]
    shapes, ospecs = _tail_out(B, nt - off)
    return pl.pallas_call(
        _rwkv_post_kernel, out_shape=shapes, grid=(B, nt - off),
        in_specs=ydir + [tile, tile, tile, _mod_spec(layer, B, ctx_out)] + [_full(a.shape) for a in consts],
        out_specs=ospecs,
        compiler_params=_cparams(("parallel", "parallel")),
        name="rwkv_post",
    )(y, y, bonus, g, x, mods, *consts)


def _sgu_kernel(x, f, modp, modc, n1g, win, bin_, lng, lnb, ws, bs, wout, n2g, wrt, x_o, h_o, lg_o):
    x1 = x[0] + modp[0, 0, 5:6, :] * f[0]
    m = modc[0, 0]
    h = _adanorm(x1, n1g[...], m[0:1], m[1:2])
    z = _dot(h, win[...]) + bin_[...]
    z = 0.5 * z * (1.0 + lax.erf(z * 0.7071067811865476))
    u, v = z[:, :SG_HALF], z[:, SG_HALF:]
    mu = jnp.mean(v, axis=-1, keepdims=True)
    dv = v - mu
    var = jnp.mean(dv * dv, axis=-1, keepdims=True)
    vn = (dv * lax.rsqrt(var + LN_EPS) * lng[...] + lnb[...]).astype(BF16)
    gw = SG_HALF // SG_GROUPS
    rows = []
    for c in range(TM // SG_CHUNK):
        rs = slice(c * SG_CHUNK, (c + 1) * SG_CHUNK)
        cols = []
        for gi in range(SG_GROUPS):
            cs = slice(gi * gw, (gi + 1) * gw)
            vs = jnp.dot(ws[gi], vn[rs, cs], preferred_element_type=F32) + bs[gi]
            cols.append((u[rs, cs] * vs).astype(BF16))
        rows.append(jnp.concatenate(cols, axis=1))
    gated = jnp.concatenate(rows, axis=0)
    _mixer_tail(x1, _dot(gated, wout[...]), m, n2g, wrt, x_o, h_o, lg_o)


def _sgu_call(x, f, mods, layer, prm, n2g, wrt, has_ctx):
    B, T, _ = x.shape
    nt = T // TM
    tile = pl.BlockSpec((1, TM, D), lambda b, j: (b, j, 0))
    consts = [prm["n1g"], prm["win"], prm["bin"], prm["lng"], prm["lnb"], prm["ws"], prm["bs"], prm["wout"],
              n2g, wrt]
    shapes, ospecs = _tail_out(B, nt)
    return pl.pallas_call(
        _sgu_kernel, out_shape=shapes, grid=(B, nt),
        in_specs=[tile, tile, _mod_spec(layer - 1, B, has_ctx), _mod_spec(layer, B, has_ctx)]
        + [_full(a.shape) for a in consts],
        out_specs=ospecs,
        compiler_params=_cparams(("parallel", "parallel")),
        name="chunk_sgu",
    )(x, f, mods, mods, *consts)


def _route_kernel(lg_ref, rb_ref, slot_ref, wt_ref, cnt_ref, *, W):
    E, G = N_EXPERTS, N_GROUPS
    per = E // G
    s = _sigmoid(lg_ref[0])
    biased = s + rb_ref[:, 0:1]
    eidx = lax.broadcasted_iota(jnp.int32, (E, W), 0)

    best = None
    for gi in range(G):
        a, b, c, d = (biased[per * gi + i:per * gi + i + 1, :] for i in range(per))
        hi1, lo1, hi2, lo2 = jnp.maximum(a, b), jnp.minimum(a, b), jnp.maximum(c, d), jnp.minimum(c, d)
        top2 = jnp.maximum(hi1, hi2) + jnp.maximum(jnp.minimum(hi1, hi2), jnp.maximum(lo1, lo2))
        if best is None:
            best, bg = top2, jnp.zeros((1, W), jnp.int32)
        else:
            upd = top2 > best
            bg = jnp.where(upd, gi, bg)
            best = jnp.where(upd, top2, best)
    neg = -jnp.inf
    m1 = jnp.where((eidx // per) == bg, biased, neg)
    i1 = jnp.min(jnp.where(m1 == jnp.max(m1, axis=0, keepdims=True), eidx, E), axis=0, keepdims=True)
    sel1 = eidx == i1
    m2 = jnp.where(sel1, neg, m1)
    i2 = jnp.min(jnp.where(m2 == jnp.max(m2, axis=0, keepdims=True), eidx, E), axis=0, keepdims=True)
    sel2 = eidx == i2
    w1 = jnp.sum(jnp.where(sel1, s, 0.0), axis=0, keepdims=True)
    w2 = jnp.sum(jnp.where(sel2, s, 0.0), axis=0, keepdims=True)
    wt_ref[0] = jnp.concatenate([w1 / (w1 + w2), w2 / (w1 + w2)], axis=0)

    onehot = jnp.where(sel1 | sel2, 1.0, 0.0)
    blk = MOE_BLOCK
    upper = jnp.where(lax.broadcasted_iota(jnp.int32, (blk, blk), 0) < lax.broadcasted_iota(jnp.int32, (blk, blk), 1),
                      1.0, 0.0).astype(BF16)
    slots, sizes = [], []
    for i in range(W // blk):
        cols = slice(i * blk, (i + 1) * blk)
        ob = onehot[:, cols]
        rank = jnp.dot(ob.astype(BF16), upper, preferred_element_type=F32)
        size = jnp.ceil(jnp.sum(ob, axis=1, keepdims=True) * (1.0 / MOE_G)) * MOE_G
        offs, cum = [], jnp.zeros((1, 1), F32)
        for e in range(E):
            offs.append(cum)
            cum = cum + size[e:e + 1, :]
        pos = jnp.concatenate(offs, axis=0) + rank
        slots.append(jnp.concatenate([jnp.sum(jnp.where(sel1[:, cols], pos, 0.0), axis=0, keepdims=True),
                                      jnp.sum(jnp.where(sel2[:, cols], pos, 0.0), axis=0, keepdims=True)], axis=0))
        sizes.append(size)
    slot_ref[0] = jnp.concatenate(slots, axis=1).astype(jnp.int32)
    sizes.append(jnp.zeros((E, 128 - len(sizes)), F32))
    cnt_ref[0] = jnp.concatenate(sizes, axis=1).astype(jnp.int32)


def _route_call(logits, rbias):
    B, _, W = logits.shape
    win = lambda w: (w, 0, 0)
    return pl.pallas_call(
        functools.partial(_route_kernel, W=W),
        out_shape=[jax.ShapeDtypeStruct((B, 2, W), jnp.int32), jax.ShapeDtypeStruct((B, 2, W), F32),
                   jax.ShapeDtypeStruct((B, N_EXPERTS, 128), jnp.int32)],
        grid=(B,),
        in_specs=[pl.BlockSpec((1, N_EXPERTS, W), win), _full(rbias.shape)],
        out_specs=[pl.BlockSpec((1, 2, W), win), pl.BlockSpec((1, 2, W), win),
                   pl.BlockSpec((1, N_EXPERTS, 128), win)],
        compiler_params=_cparams(("parallel",)),
        name="moe_route",
    )(logits, rbias)


def _dispatch_tables(sizes, nb):
    max_rows = nb * MOE_CAP
    max_tiles = -(-max_rows // MOE_TM) + N_EXPERTS
    max_tiles = -(-max_tiles // MOE_TPS) * MOE_TPS
    rows = sizes.sum(-1)
    rows_p = -(-rows // MOE_TM) * MOE_TM
    end = jnp.cumsum(rows_p, axis=1)
    start = end - rows_p
    n_tiles = end[:, -1] // MOE_TM
    experts = jnp.arange(N_EXPERTS)
    last = jnp.max(jnp.where(rows > 0, experts[None, :], 0), axis=1, keepdims=True)

    t_row = jnp.arange(max_tiles) * MOE_TM
    owns = (start[:, :, None] <= t_row) & (t_row < end[:, :, None])
    texp = jnp.where(t_row[None, :] < end[:, -1:], jnp.sum(jnp.where(owns, experts[None, :, None], 0), axis=1), last)

    q_row = jnp.arange(max_tiles * MOE_TM // MOE_G) * MOE_G
    seg_start = start[:, :, None] + jnp.cumsum(sizes, axis=2) - sizes
    local = jnp.arange(nb)[None, None, :] * MOE_CAP + jnp.cumsum(sizes, axis=1) - sizes
    hit = (seg_start[..., None] <= q_row) & (q_row < (seg_start + sizes)[..., None])
    src_row = jnp.sum(jnp.where(hit, q_row + (local - seg_start)[..., None], 0), axis=(1, 2))
    table = jnp.where(jnp.any(hit, axis=(1, 2)), src_row // MOE_G, nb * MOE_CAP // MOE_G).astype(jnp.int32)
    return texp.astype(jnp.int32).reshape(-1), n_tiles.astype(jnp.int32), table[:, None, :], max_tiles


def _moe_kernel(texp_ref, nt_ref, h_ref, slot_ref, col_ref, tab_ref, *refs, nb):
    weights, (o_ref, rows_ref, xt_ref) = refs[:2 * MOE_TPS], refs[2 * MOE_TPS:]
    w = pl.program_id(0)
    t = pl.program_id(1)
    cap, gpt = MOE_CAP, MOE_TM // MOE_G
    zero_group = nb * cap // MOE_G

    @pl.when(t == 0)
    def _():
        rows_ref[nb * cap:nb * cap + 2 * MOE_G, :] = jnp.zeros((2 * MOE_G, D), rows_ref.dtype)
        rid = lax.broadcasted_iota(jnp.int32, (cap, MOE_BLOCK), 0)
        for b in range(nb):
            cols = slice(b * MOE_BLOCK, (b + 1) * MOE_BLOCK)
            hit = (slot_ref[0, 0:1, cols] == rid) | (slot_ref[0, 1:2, cols] == rid)
            disp = jnp.where(hit, 1.0, 0.0).astype(BF16)
            rows_ref[b * cap:(b + 1) * cap, :] = jnp.dot(
                disp, h_ref[0, cols, :], preferred_element_type=F32).astype(rows_ref.dtype)

    def expert_tile(tile, wgu_ref, wd_ref):
        src = [tab_ref[0, 0, tile * gpt + g] for g in range(gpt)]
        for g in range(gpt):
            xt_ref[g * MOE_G:(g + 1) * MOE_G, :] = rows_ref[pl.ds(pl.multiple_of(src[g] * MOE_G, MOE_G), MOE_G), :]
        gu = jnp.dot(xt_ref[...], wgu_ref[0], preferred_element_type=F32)
        gate, up = gu[:, :EXPERT_FFN], gu[:, EXPERT_FFN:]
        y = _dot(gate * _sigmoid(gate) * up, wd_ref[0]).astype(rows_ref.dtype)
        for g in range(gpt):
            dst = jnp.where(src[g] == zero_group, zero_group + 1, src[g])
            rows_ref[pl.ds(pl.multiple_of(dst * MOE_G, MOE_G), MOE_G), :] = y[g * MOE_G:(g + 1) * MOE_G, :]

    for i in range(MOE_TPS):
        tile = t * MOE_TPS + i
        pl.when(tile < nt_ref[w])(functools.partial(expert_tile, tile, weights[2 * i], weights[2 * i + 1]))

    @pl.when(t == pl.num_programs(1) - 1)
    def _():
        cid = lax.broadcasted_iota(jnp.int32, (MOE_BLOCK, cap), 1).astype(F32)
        for b in range(nb):
            col = col_ref[0, b * MOE_BLOCK:(b + 1) * MOE_BLOCK, :]
            comb = (jnp.where(col[:, 0:1] == cid, col[:, 2:3], 0.0)
                    + jnp.where(col[:, 1:2] == cid, col[:, 3:4], 0.0)).astype(BF16)
            o_ref[0, b * MOE_BLOCK:(b + 1) * MOE_BLOCK, :] = jnp.dot(
                comb, rows_ref[b * cap:(b + 1) * cap, :], preferred_element_type=F32).astype(o_ref.dtype)


def _moe_call(h2, slots, colinfo, texp, ntiles, table, max_tiles, wgu, wd):
    nw, W, _ = h2.shape
    nb = W // MOE_BLOCK
    win = lambda w, t, te, nt: (w, 0, 0)
    weight_specs = []
    for i in range(MOE_TPS):
        expert = functools.partial(lambda w, t, te, nt, i: (te[w * max_tiles + t * MOE_TPS + i], 0, 0), i=i)
        weight_specs += [pl.BlockSpec((1, D, 2 * EXPERT_FFN), expert), pl.BlockSpec((1, EXPERT_FFN, D), expert)]
    grid_spec = pltpu.PrefetchScalarGridSpec(
        num_scalar_prefetch=2, grid=(nw, max_tiles // MOE_TPS),
        in_specs=[pl.BlockSpec((1, W, D), win), pl.BlockSpec((1, 2, W), win), pl.BlockSpec((1, W, 4), win),
                  pl.BlockSpec((1, 1, table.shape[-1]), win, memory_space=pltpu.SMEM)] + weight_specs,
        out_specs=pl.BlockSpec((1, W, D), win),
        scratch_shapes=[pltpu.VMEM((nb * MOE_CAP + 2 * MOE_G, D), BF16), pltpu.VMEM((MOE_TM, D), BF16)])
    return pl.pallas_call(
        functools.partial(_moe_kernel, nb=nb),
        out_shape=jax.ShapeDtypeStruct((nw, W, D), BF16),
        grid_spec=grid_spec,
        compiler_params=_cparams(("parallel", "arbitrary")),
        name="moe_experts",
    )(texp, ntiles, h2, slots, colinfo, table, *([wgu, wd] * MOE_TPS))


def _moe_layer(h2, logits, rbias, wgu, wd):
    nb = h2.shape[1] // MOE_BLOCK
    slots, wts, sizes = _route_call(logits, rbias)
    texp, ntiles, table, max_tiles = _dispatch_tables(sizes[:, :, :nb], nb)
    colinfo = jnp.concatenate([slots.astype(F32), wts], axis=1).transpose(0, 2, 1)
    return _moe_call(h2, slots, colinfo, texp, ntiles, table, max_tiles, wgu, wd)


def _final_kernel(x, f, modp, g, o):
    o[0] = _rmsnorm(x[0] + modp[0, 0, 5:6, :] * f[0]) * g[...]


def _final_call(x, f, mods, layer, final_g):
    B, T, _ = x.shape
    tile = pl.BlockSpec((1, TM, D), lambda b, j: (b, j, 0))
    return pl.pallas_call(
        _final_kernel, out_shape=jax.ShapeDtypeStruct((B, T, D), F32), grid=(B, T // TM),
        in_specs=[tile, tile, _mod_spec(layer, B, False), _full(final_g.shape)], out_specs=tile,
        compiler_params=_cparams(("parallel", "parallel")),
        name="final_norm",
    )(x, f, mods, final_g)


def _rwkv_params(a, norm1_g, layer, rw_mu, rw_wr, rw_wk, rw_wv, rw_wo, rw_w0, rw_w1, rw_w2, rw_a0, rw_a1, rw_a2,
                 rw_g1, rw_g2, rw_kk, rw_ka, rw_rk, rw_lnx_g, rw_lnx_b, epair):
    lora = rw_w1.shape[-1]

    def block_diag(w2):
        z = jnp.zeros((lora, D), w2.dtype)
        return jnp.concatenate([jnp.concatenate([w2[0], z], axis=1), jnp.concatenate([z, w2[1]], axis=1)], axis=0)

    return dict(
        n1g=norm1_g[layer].reshape(1, D), mu=rw_mu[a],
        wr=rw_wr[a].astype(BF16), wk=rw_wk[a].astype(BF16), wv=rw_wv[a].astype(BF16), wo=rw_wo[a].astype(BF16),
        w1c=jnp.concatenate([rw_w1[a, 0], rw_w1[a, 1]], axis=1).astype(BF16), w2b=block_diag(rw_w2[a]).astype(BF16),
        w0=rw_w0[a],
        a1c=jnp.concatenate([rw_a1[a, 0], rw_a1[a, 1]], axis=1).astype(BF16), a2b=block_diag(rw_a2[a]).astype(BF16),
        a0=rw_a0[a],
        g1=rw_g1[a].astype(BF16), g2=rw_g2[a].astype(BF16),
        kk=rw_kk[a].reshape(1, D), ka=rw_ka[a].reshape(1, D), rk=rw_rk[a].reshape(1, D),
        lng=rw_lnx_g[a].reshape(1, D), lnb=rw_lnx_b[a].reshape(1, D), epair=epair)


def kernel(x, c, ctx, c_ctx, w_mod, b_mod, norm1_g, norm2_g, rw_mu, rw_wr, rw_wk, rw_wv, rw_wo, rw_w0, rw_w1, rw_w2, rw_a0, rw_a1, rw_a2, rw_v0, rw_v1, rw_v2, rw_g1, rw_g2, rw_kk, rw_ka, rw_rk, rw_lnx_g, rw_lnx_b, sg_w_in, sg_b_in, sg_ln_g, sg_ln_b, sg_w_s, sg_b_s, sg_w_out, w_router, router_bias, ex_w_gate, ex_w_up, ex_w_down, final_g):
    B, L, _ = x.shape
    Lc = ctx.shape[1]
    assert Lc == TM and L % TM == 0 and B + 1 <= MOD_ROWS and w_mod.shape[0] == 4

    cvec = jnp.concatenate([c, c_ctx[None, :], jnp.zeros((MOD_ROWS - B - 1, D), F32)], axis=0)
    mods = _mod_call(cvec, w_mod, b_mod).reshape(w_mod.shape[0], MOD_ROWS, N_MOD, D)
    xs = jnp.concatenate([ctx, x], axis=1)

    lane = jnp.arange(128) // HEAD
    epair = (lane[:, None] == lane[None, :]).astype(BF16)
    wrt = w_router.T
    rbias = jnp.broadcast_to(router_bias[:, None], (N_EXPERTS, 128))
    n2g = [norm2_g[i].reshape(1, D) for i in range(4)]
    wgu = [jnp.concatenate([ex_w_gate[i], ex_w_up[i]], axis=-1).astype(BF16) for i in range(4)]
    wd = [ex_w_down[i].astype(BF16) for i in range(4)]
    rw_common = (rw_mu, rw_wr, rw_wk, rw_wv, rw_wo, rw_w0, rw_w1, rw_w2, rw_a0, rw_a1, rw_a2, rw_g1, rw_g2,
                 rw_kk, rw_ka, rw_rk, rw_lnx_g, rw_lnx_b, epair)

    def sgu_params(b, layer):
        return dict(n1g=norm1_g[layer].reshape(1, D), win=sg_w_in[b].astype(BF16), bin=sg_b_in[b].reshape(1, -1),
                    lng=sg_ln_g[b].reshape(1, -1), lnb=sg_ln_b[b].reshape(1, -1), ws=sg_w_s[b].astype(BF16),
                    bs=jnp.broadcast_to(sg_b_s[b][:, :, None], (SG_GROUPS, SG_CHUNK, SG_HALF // SG_GROUPS)),
                    wout=sg_w_out[b].astype(BF16))

    p0 = _rwkv_params(0, norm1_g, 0, *rw_common)
    k0, kap0, v0, r0, a0, lam0, g0, bon0 = _rwkv_pre_call(xs, None, mods, 0, p0, None, None)
    y0 = _wkv_call(k0, kap0, v0, r0, a0, lam0, p0["ka"].reshape(PAIRS, 1, 128), Lc)
    x1, h1, lg1 = _rwkv_post_call(y0, bon0, g0, xs, mods, 0, p0, n2g[0], wrt, True)
    f0 = _moe_layer(h1, lg1, rbias, wgu[0], wd[0])

    x2, h2, lg2 = _sgu_call(x1, f0, mods, 1, sgu_params(0, 1), n2g[1], wrt, True)
    f1 = _moe_layer(h2, lg2, rbias, wgu[1], wd[1])

    p2 = _rwkv_params(1, norm1_g, 2, *rw_common)
    vres = dict(v0=rw_v0[0].reshape(1, D),
                v1=jnp.pad(rw_v1[0], ((0, 0), (0, 128 - rw_v1.shape[-1]))).astype(BF16),
                v2=jnp.pad(rw_v2[0], ((0, 128 - rw_v2.shape[-2]), (0, 0))).astype(BF16))
    k2, kap2, v2, r2, a2, lam2, g2, bon2, x2r = _rwkv_pre_call(x2, f1, mods, 2, p2, vres, v0)
    y2 = _wkv_call(k2, kap2, v2, r2, a2, lam2, p2["ka"].reshape(PAIRS, 1, 128), Lc)
    x3, h3, lg3 = _rwkv_post_call(y2, bon2, g2, x2r, mods, 2, p2, n2g[2], wrt, False)
    f2 = _moe_layer(h3, lg3, rbias, wgu[2], wd[2])

    x4, h4, lg4 = _sgu_call(x3, f2, mods, 3, sgu_params(1, 3), n2g[3], wrt, False)
    f3 = _moe_layer(h4, lg4, rbias, wgu[3], wd[3])
    return _final_call(x4, f3, mods, 3, final_g.reshape(1, D))
```

```python
import functools

import jax
import jax.numpy as jnp
from jax import lax
from jax.experimental import pallas as pl
from jax.experimental.pallas import tpu as pltpu

F32 = jnp.float32
BF16 = jnp.bfloat16

D = 1024
HEAD = 64
PAIRS = D // 128
GRID_W = 64
N_MOD = 6
N_EXPERTS = 16
N_GROUPS = 4
EXPERT_FFN = 512
SG_CHUNK = 128
SG_GROUPS = 16
SG_HALF = 2048
NORM_EPS = 1e-6
LN_EPS = 1e-5
GN_EPS = 64e-5
EXP_NEG_HALF = 0.6065306597126334

TM = 256
HALO = GRID_W
WKV_C = 64
WKV_GROUP = 8
WKV_SUB = 4
MOE_TM = 192
MOE_TPS = 2
MOE_BLOCK = 256
MOE_G = 16
MOE_CAP = 2 * MOE_BLOCK + N_EXPERTS * (MOE_G - 1)
MOD_ROWS = 40
VMEM_LIMIT = 56 * 1024 * 1024


def _cparams(sem):
    return pltpu.CompilerParams(dimension_semantics=sem, vmem_limit_bytes=VMEM_LIMIT)


def _dot(a, b):
    return jnp.dot(a.astype(BF16), b.astype(BF16), preferred_element_type=F32)


def _dot_nt(a, b):
    return lax.dot_general(a.astype(BF16), b.astype(BF16), (((1,), (1,)), ((), ())),
                           preferred_element_type=F32)


def _dot_tn(a, b):
    return lax.dot_general(a.astype(BF16), b.astype(BF16), (((0,), (0,)), ((), ())),
                           preferred_element_type=F32)


def _sigmoid(x):
    return 1.0 / (1.0 + jnp.exp(-x))


def _rmsnorm(x):
    return x * lax.rsqrt(jnp.mean(x * x, axis=-1, keepdims=True) + NORM_EPS)


def _adanorm(x, g, shift, scale):
    return _rmsnorm(x) * g * (1.0 + scale) + shift


def _full(shape):
    zeros = (0,) * len(shape)
    return pl.BlockSpec(shape, lambda *_: zeros)


def _mod_kernel(c_ref, w_ref, b_ref, o_ref):
    c = c_ref[...]
    o_ref[0] = jnp.dot(c * _sigmoid(c), w_ref[0], preferred_element_type=F32) + b_ref[0]


def _mod_call(cvec, w_mod, b_mod):
    depth = w_mod.shape[0]
    return pl.pallas_call(
        _mod_kernel,
        out_shape=jax.ShapeDtypeStruct((depth, MOD_ROWS, N_MOD * D), F32),
        grid=(depth, N_MOD),
        in_specs=[pl.BlockSpec((MOD_ROWS, D), lambda i, n: (0, 0)),
                  pl.BlockSpec((1, D, D), lambda i, n: (i, 0, n)),
                  pl.BlockSpec((1, 1, D), lambda i, n: (i, 0, n))],
        out_specs=pl.BlockSpec((1, MOD_ROWS, D), lambda i, n: (i, 0, n)),
        compiler_params=_cparams(("parallel", "parallel")),
        name="modulation",
    )(cvec, w_mod, b_mod.reshape(depth, 1, N_MOD * D))


def _mod_spec(layer, batch, ctx_tile):
    if ctx_tile:
        return pl.BlockSpec((1, 1, N_MOD, D), lambda b, j: (layer, jnp.where(j == 0, batch, b), 0, 0))
    return pl.BlockSpec((1, 1, N_MOD, D), lambda b, j: (layer, b, 0, 0))


def _rwkv_pre_kernel(*refs, has_f, has_vres, n_tiles):
    it = iter(refs)
    xm, xp, xn = next(it), next(it), next(it)
    if has_f:
        fm, fp, fn, modp = next(it), next(it), next(it), next(it)
    modc, n1g, mu = next(it), next(it), next(it)
    wr, wk, wv = next(it), next(it), next(it)
    w1c, w2b, w0 = next(it), next(it), next(it)
    a1c, a2b, a0 = next(it), next(it), next(it)
    g1, g2 = next(it), next(it)
    if has_vres:
        v0, v1, v2, vf = next(it), next(it), next(it), next(it)
    kkv, kav, rkv, ep = next(it), next(it), next(it), next(it)
    k_o, kap_o, v_o, r_o, a_o, lam_o, g_o, bon_o = (next(it) for _ in range(8))
    if has_f:
        x_o = next(it)

    j = pl.program_id(1)
    x_all = jnp.concatenate([xp[0], xm[0], xn[0]], axis=0)
    if has_f:
        f_all = jnp.concatenate([fp[0], fm[0], fn[0]], axis=0)
        x_all = x_all + modp[0, 0, 5:6, :] * f_all
        x_o[0] = x_all[HALO:HALO + TM]
    m = modc[0, 0]
    h_all = _adanorm(x_all, n1g[...], m[0:1], m[1:2])
    hm = h_all[HALO:HALO + TM]

    q = D // 4
    r = lax.broadcasted_iota(jnp.int32, (TM, q), 0)
    jv = jnp.zeros((TM, q), jnp.int32) + j
    is_ctx = jv == 0
    colp = r % GRID_W
    prev = pltpu.roll(hm, 1, axis=0)
    nxt = pltpu.roll(hm, TM - 1, axis=0)
    up = h_all[0:TM]
    down = h_all[2 * HALO:2 * HALO + TM]
    first, last = r == 0, r == TM - 1
    q0 = jnp.where((is_ctx & first) | (~is_ctx & (colp == 0)), 0.0, prev[:, 0:q])
    q1 = jnp.where(is_ctx, jnp.where(first, 0.0, prev[:, q:2 * q]),
                   jnp.where(colp == GRID_W - 1, 0.0, nxt[:, q:2 * q]))
    q2 = jnp.where(is_ctx, jnp.where(last, 0.0, nxt[:, 2 * q:3 * q]),
                   jnp.where((jv == 1) & (r < HALO), 0.0, up[:, 2 * q:3 * q]))
    q3 = jnp.where(is_ctx, jnp.where(last, 0.0, nxt[:, 3 * q:]),
                   jnp.where((jv == n_tiles - 1) & (r >= TM - HALO), 0.0, down[:, 3 * q:]))
    xx = jnp.concatenate([q0, q1, q2, q3], axis=1) - hm

    hm_b, xx_b, mu_b = hm.astype(BF16), xx.astype(BF16), mu[...].astype(BF16)

    def mix(i):
        return hm_b + xx_b * mu_b[i:i + 1, :]

    xr, xw, xk, xv, xa, xg = (mix(i) for i in range(6))
    rr = _dot(xr, wr[...])
    kk = _dot(xk, wk[...])
    vv = _dot(xv, wv[...])
    if has_vres:
        gate = _sigmoid(v0[...] + _dot(_dot(xv, v1[...]), v2[...]))
        vfirst = jnp.concatenate([vf[0, p].astype(F32) for p in range(PAIRS)], axis=1)
        vv = vv + (vfirst - vv) * gate
    g_o[0] = _dot(_sigmoid(_dot(xg, g1[...])), g2[...]).astype(g_o.dtype)

    zw = _dot(jnp.tanh(_dot(xw, w1c[...])), w2b[...])
    za = _dot(_dot(xa, a1c[...]), a2b[...])
    ka = kav[...]
    kd_sum = jnp.zeros((TM, D), F32)
    for d in range(2):
        z = w0[d:d + 1, :] + zw[:, d * D:(d + 1) * D]
        lam = -EXP_NEG_HALF * _sigmoid(z)
        a = _sigmoid(a0[d:d + 1, :] + za[:, d * D:(d + 1) * D])
        kd_sum = kd_sum + kk * (1.0 + (a - 1.0) * ka)
        for p in range(PAIRS):
            sl = slice(128 * p, 128 * (p + 1))
            lam_o[d, 0, p] = lam[:, sl]
            a_o[d, 0, p] = a[:, sl].astype(a_o.dtype)

    kkk = kk * kkv[...]
    rk = rr * rkv[...] * kd_sum
    for p in range(PAIRS):
        sl = slice(128 * p, 128 * (p + 1))
        kp = kkk[:, sl]
        ss = _dot(kp * kp, ep[...])
        kap_o[0, p] = (kp * lax.rsqrt(jnp.maximum(ss, 1e-24))).astype(kap_o.dtype)
        bon_o[0, :, sl] = (_dot(rk[:, sl], ep[...]) * vv[:, sl]).astype(bon_o.dtype)
        k_o[0, p] = kk[:, sl].astype(k_o.dtype)
        v_o[0, p] = vv[:, sl].astype(v_o.dtype)
        r_o[0, p] = rr[:, sl].astype(r_o.dtype)


def _rwkv_pre_call(xs, f_prev, mods, layer, prm, vres, v_first):
    B, T, _ = xs.shape
    nt = T // TM
    nh = T // HALO
    has_f = f_prev is not None
    has_vres = vres is not None
    main = pl.BlockSpec((1, TM, D), lambda b, j: (b, j, 0))
    prev = pl.BlockSpec((1, HALO, D), lambda b, j: (b, jnp.maximum(j * (TM // HALO) - 1, 0), 0))
    nxt = pl.BlockSpec((1, HALO, D), lambda b, j: (b, jnp.minimum((j + 1) * (TM // HALO), nh - 1), 0))
    pm = pl.BlockSpec((1, PAIRS, TM, 128), lambda b, j: (b, 0, j, 0))
    pm2 = pl.BlockSpec((2, 1, PAIRS, TM, 128), lambda b, j: (0, b, 0, j, 0))

    args, specs = [xs, xs, xs], [main, prev, nxt]
    if has_f:
        args += [f_prev, f_prev, f_prev, mods]
        specs += [main, prev, nxt, _mod_spec(layer - 1, B, True)]
    args += [mods, prm["n1g"], prm["mu"], prm["wr"], prm["wk"], prm["wv"], prm["w1c"], prm["w2b"], prm["w0"],
             prm["a1c"], prm["a2b"], prm["a0"], prm["g1"], prm["g2"]]
    specs += [_mod_spec(layer, B, True)] + [_full(a.shape) for a in args[len(specs) + 1:]]
    if has_vres:
        args += [vres["v0"], vres["v1"], vres["v2"], v_first]
        specs += [_full(vres["v0"].shape), _full(vres["v1"].shape), _full(vres["v2"].shape), pm]
    tail = [prm["kk"], prm["ka"], prm["rk"], prm["epair"]]
    args += tail
    specs += [_full(a.shape) for a in tail]

    pm_shape = jax.ShapeDtypeStruct((B, PAIRS, T, 128), BF16)
    out_shape = [pm_shape, pm_shape, pm_shape, pm_shape,
                 jax.ShapeDtypeStruct((2, B, PAIRS, T, 128), BF16),
                 jax.ShapeDtypeStruct((2, B, PAIRS, T, 128), F32),
                 jax.ShapeDtypeStruct((B, T, D), BF16),
                 jax.ShapeDtypeStruct((B, T, D), BF16)]
    out_specs = [pm, pm, pm, pm, pm2, pm2, main, main]
    if has_f:
        out_shape.append(jax.ShapeDtypeStruct((B, T, D), F32))
        out_specs.append(main)
    return pl.pallas_call(
        functools.partial(_rwkv_pre_kernel, has_f=has_f, has_vres=has_vres, n_tiles=nt),
        out_shape=out_shape, grid=(B, nt), in_specs=specs, out_specs=out_specs,
        compiler_params=_cparams(("parallel", "parallel")),
        name="rwkv_pre",
    )(*args)


def _wkv_kernel(k_ref, kap_ref, v_ref, r_ref, a_ref, lam_ref, ka_ref, y_ref, s_ref):
    C = WKV_C
    d = pl.program_id(1)
    j = pl.program_id(2)

    @pl.when(j == 0)
    def _():
        s_ref[...] = jnp.zeros_like(s_ref)

    dv = jnp.zeros((C, 128), jnp.int32) + d
    tr = lax.broadcasted_iota(jnp.int32, (C, 128), 0)
    lane = lax.broadcasted_iota(jnp.int32, (C, 128), 1)
    tc = lane % C
    strict = ((dv == 0) & (tc < tr)) | ((dv != 0) & (tc > tr))
    incl = strict | (tc == tr)
    eye_ld = tc == tr
    cum_mask = jnp.where(incl[:, :C], 1.0, 0.0).astype(BF16)
    lane_lo = lane < HEAD
    row2 = lax.broadcasted_iota(jnp.int32, (128, 128), 0)
    col2 = lax.broadcasted_iota(jnp.int32, (128, 128), 1)
    same = (row2 // HEAD) == (col2 // HEAD)
    eye = row2 == col2

    def stack(x):
        z = jnp.zeros_like(x)
        return jnp.concatenate([jnp.where(lane_lo, x, z), jnp.where(lane_lo, z, x)], axis=0)

    offs = [pl.multiple_of(jnp.where(d == 0, q * C, (WKV_SUB - 1 - q) * C), C) for q in range(WKV_SUB)]

    def group(ps):
        def each(fn, *lists):
            return [fn(*xs) for xs in zip(*lists)]

        inst = [(p, q) for q in range(WKV_SUB) for p in ps]
        rows = [pl.ds(offs[q], C) for _, q in inst]
        lam = [lam_ref[0, 0, p, rw, :] for (p, _), rw in zip(inst, rows)]
        kap = [kap_ref[0, p, rw, :].astype(F32) for (p, _), rw in zip(inst, rows)]
        v = [v_ref[0, p, rw, :] for (p, _), rw in zip(inst, rows)]
        v_z = [stack(x) for x in v]
        a = [a_ref[0, 0, p, rw, :].astype(F32) for (p, _), rw in zip(inst, rows)]
        kd = [k_ref[0, p, rw, :].astype(F32) * (1.0 + (ai - 1.0) * ka_ref[p])
              for (p, _), rw, ai in zip(inst, rows, a)]
        bet = each(lambda x, y: x * y, kap, a)

        def prefix(l):
            hi = l.astype(BF16)
            lo = (l - hi.astype(F32)).astype(BF16)
            cs = jnp.dot(cum_mask, jnp.concatenate([hi, lo], axis=1), preferred_element_type=F32)
            return cs[:, :128] + cs[:, 128:]

        lc = each(prefix, lam)
        ltot = [jnp.sum(l, axis=0, keepdims=True) for l in lam]
        e_neg = [jnp.exp(-x) for x in lc]
        qk = each(lambda kp, c, l: (kp * jnp.exp(c - l)).astype(BF16), kap, lc, lam)
        rt = [r_ref[0, p, rw, :].astype(F32) * jnp.exp(c) for (p, _), rw, c in zip(inst, rows, lc)]

        def scores(q, ri, kdi, bi, en):
            kt = (kdi * en).astype(BF16)
            bt = (bi * en).astype(BF16)
            return _dot_nt(jnp.concatenate([q, ri.astype(BF16)], axis=0),
                           jnp.concatenate([stack(kt), stack(bt)], axis=0))

        sc = each(scores, qk, rt, kd, bet, e_neg)
        m_k = [jnp.where(strict, s[:C, :128], 0.0).astype(BF16) for s in sc]
        a_kb = [jnp.concatenate([jnp.where(incl, s[C:, :128], 0.0), jnp.where(incl, -s[C:, 128:], 0.0)],
                                axis=1).astype(BF16) for s in sc]
        mkv = each(_dot, m_k, v_z)

        levels = C.bit_length() - 1
        pw = [jnp.where(strict, -s[:C, 128:], 0.0).astype(BF16) for s in sc]
        t_inv = [jnp.where(eye_ld, 1.0, 0.0) + x.astype(F32) for x in pw]
        pw = [_dot(x, stack(x)).astype(BF16) for x in pw]
        for _ in range(1, levels - 1):
            both = each(lambda x, t: _dot(x, jnp.concatenate([stack(x), stack(t.astype(BF16))], axis=1)), pw, t_inv)
            pw = [x[:, :128].astype(BF16) for x in both]
            t_inv = each(lambda t, x: t + x[:, 128:], t_inv, both)
        t_inv = each(lambda t, x: t + _dot(x, stack(t.astype(BF16))), t_inv, pw)

        tw = each(lambda t, q, m: _dot(t, jnp.concatenate([stack(q), stack(m.astype(BF16))], axis=1)).astype(BF16),
                  t_inv, qk, mkv)
        yr = each(lambda ab, vz, t: _dot(ab, jnp.concatenate(
            [jnp.concatenate([vz, jnp.zeros_like(vz)], axis=1),
             jnp.concatenate([stack(t[:, 128:]), stack(t[:, :128])], axis=1)], axis=0)), a_kb, v_z, tw)
        e_tail = each(lambda lt, c: jnp.exp(lt - c), ltot, lc)
        hg = each(lambda x, b, e, vi, t: _dot_tn(
            jnp.concatenate([x * e, -b * e], axis=0),
            jnp.concatenate([jnp.concatenate([vi, jnp.zeros_like(vi)], axis=1),
                             jnp.concatenate([t[:, 128:], t[:, :128]], axis=1)], axis=0)), kd, bet, e_tail, v, tw)
        rhat = each(lambda ri, x: ri + x[:, 128:], rt, yr)
        g = each(lambda lt, x: jnp.where(eye, jnp.exp(lt), 0.0) + jnp.where(same, x[:, 128:], 0.0), ltot, hg)

        def advance(s0, gi, rh):
            s_hi = s0.astype(BF16)
            s_lo = (s0 - s_hi.astype(F32)).astype(BF16)
            gs = _dot(jnp.concatenate([gi, rh], axis=0), jnp.concatenate([s_hi, s_lo], axis=1))
            return gs[:, :128] + gs[:, 128:]

        state = [s_ref[p] for p in ps]
        n = len(ps)
        for q in range(WKV_SUB):
            sl = slice(q * n, (q + 1) * n)
            gs = each(advance, state, g[sl], rhat[sl])
            state = each(lambda x, h: x[:128] + jnp.where(same, h[:, :128], 0.0), gs, hg[sl])
            for p, x, yl in zip(ps, gs, yr[sl]):
                y_ref[0, 0, p, pl.ds(offs[q], C), :] = (x[128:] + yl[:, :128]).astype(y_ref.dtype)
        for p, x in zip(ps, state):
            s_ref[p] = x

    for first in range(0, PAIRS, WKV_GROUP):
        group(range(first, first + WKV_GROUP))


def _wkv_call(k, kap, v, r, a, lam, ka, ctx_len, out_dtype=BF16):
    B, _, T, _ = k.shape
    C = WKV_C * WKV_SUB
    assert T % C == 0 and ctx_len % C == 0
    ns = T // C
    nc = ctx_len // C

    def chunk(d, j):
        rev = jnp.where(j < nc, nc - 1 - j, ns - 1 + nc - j)
        return jnp.where(d == 0, j, rev)

    tok = pl.BlockSpec((1, PAIRS, C, 128), lambda b, d, j: (b, 0, chunk(d, j), 0))
    per_dir = pl.BlockSpec((1, 1, PAIRS, C, 128), lambda b, d, j: (d, b, 0, chunk(d, j), 0))
    return pl.pallas_call(
        _wkv_kernel,
        out_shape=jax.ShapeDtypeStruct((2, B, PAIRS, T, 128), out_dtype),
        grid=(B, 2, ns),
        in_specs=[tok, tok, tok, tok, per_dir, per_dir, _full((PAIRS, 1, 128))],
        out_specs=per_dir,
        scratch_shapes=[pltpu.VMEM((PAIRS, 128, 128), F32)],
        compiler_params=_cparams(("parallel", "parallel", "arbitrary")),
        name="wkv_chunked",
    )(k, kap, v, r, a, lam, ka)


def _mixer_tail(x, mix_out, m, n2g, wrt, x_o, h_o, lg_o):
    x_new = x + m[2:3] * mix_out
    h2 = _adanorm(x_new, n2g[...], m[3:4], m[4:5])
    x_o[0] = x_new
    h_o[0] = h2.astype(h_o.dtype)
    lg_o[0] = lax.dot_general(wrt[...], h2, (((1,), (1,)), ((), ())), preferred_element_type=F32)


def _tail_out(B, nt):
    t_out = nt * TM
    shapes = [jax.ShapeDtypeStruct((B, t_out, D), F32), jax.ShapeDtypeStruct((B, t_out, D), BF16),
              jax.ShapeDtypeStruct((B, N_EXPERTS, t_out), F32)]
    tile = pl.BlockSpec((1, TM, D), lambda b, j: (b, j, 0))
    specs = [tile, tile, pl.BlockSpec((1, N_EXPERTS, TM), lambda b, j: (b, 0, j))]
    return shapes, specs


def _rwkv_post_kernel(y0, y1, bon, g, x, modc, lng, lnb, wo, n2g, wrt, ep, x_o, h_o, lg_o):
    parts = []
    for p in range(PAIRS):
        y = y0[0, 0, p].astype(F32) + y1[0, 0, p].astype(F32)
        dlt = y - _dot(y, ep[...]) * (1.0 / HEAD)
        var = _dot(dlt * dlt, ep[...]) * (1.0 / HEAD)
        parts.append(dlt * lax.rsqrt(var + GN_EPS))
    yn = jnp.concatenate(parts, axis=1)
    z = (yn * lng[...] + lnb[...] + bon[0].astype(F32)) * g[0].astype(F32)
    _mixer_tail(x[0], _dot(z, wo[...]), modc[0, 0], n2g, wrt, x_o, h_o, lg_o)


def _rwkv_post_call(y, bonus, g, x, mods, layer, prm, n2g, wrt, ctx_out):
    B, T, _ = x.shape
    nt = T // TM
    off = 0 if ctx_out else 1
    tile = pl.BlockSpec((1, TM, D), lambda b, j: (b, j + off, 0))
    ydir = [pl.BlockSpec((1, 1, PAIRS, TM, 128), functools.partial(lambda b, j, d: (d, b, 0, j + off, 0), d=d))
            for d in range(2)]
    consts = [prm["lng"], prm["lnb"], prm["wo"], n2g, wrt, prm["epair"]]
    shapes, ospecs = _tail_out(B, nt - off)
    return pl.pallas_call(
        _rwkv_post_kernel, out_shape=shapes, grid=(B, nt - off),
        in_specs=ydir + [tile, tile, tile, _mod_spec(layer, B, ctx_out)] + [_full(a.shape) for a in consts],
        out_specs=ospecs,
        compiler_params=_cparams(("parallel", "parallel")),
        name="rwkv_post",
    )(y, y, bonus, g, x, mods, *consts)


def _sgu_kernel(x, f, modp, modc, n1g, win, bin_, lng, lnb, ws, bs, wout, n2g, wrt, x_o, h_o, lg_o):
    x1 = x[0] + modp[0, 0, 5:6, :] * f[0]
    m = modc[0, 0]
    h = _adanorm(x1, n1g[...], m[0:1], m[1:2])
    z = _dot(h, win[...]) + bin_[...]
    z = 0.5 * z * (1.0 + lax.erf(z * 0.7071067811865476))
    u, v = z[:, :SG_HALF], z[:, SG_HALF:]
    mu = jnp.mean(v, axis=-1, keepdims=True)
    dv = v - mu
    var = jnp.mean(dv * dv, axis=-1, keepdims=True)
    vn = (dv * lax.rsqrt(var + LN_EPS) * lng[...] + lnb[...]).astype(BF16)
    gw = SG_HALF // SG_GROUPS
    rows = []
    for c in range(TM // SG_CHUNK):
        rs = slice(c * SG_CHUNK, (c + 1) * SG_CHUNK)
        cols = []
        for gi in range(SG_GROUPS):
            cs = slice(gi * gw, (gi + 1) * gw)
            vs = jnp.dot(ws[gi], vn[rs, cs], preferred_element_type=F32) + bs[gi]
            cols.append((u[rs, cs] * vs).astype(BF16))
        rows.append(jnp.concatenate(cols, axis=1))
    gated = jnp.concatenate(rows, axis=0)
    _mixer_tail(x1, _dot(gated, wout[...]), m, n2g, wrt, x_o, h_o, lg_o)


def _sgu_call(x, f, mods, layer, prm, n2g, wrt, has_ctx):
    B, T, _ = x.shape
    nt = T // TM
    tile = pl.BlockSpec((1, TM, D), lambda b, j: (b, j, 0))
    consts = [prm["n1g"], prm["win"], prm["bin"], prm["lng"], prm["lnb"], prm["ws"], prm["bs"], prm["wout"],
              n2g, wrt]
    shapes, ospecs = _tail_out(B, nt)
    return pl.pallas_call(
        _sgu_kernel, out_shape=shapes, grid=(B, nt),
        in_specs=[tile, tile, _mod_spec(layer - 1, B, has_ctx), _mod_spec(layer, B, has_ctx)]
        + [_full(a.shape) for a in consts],
        out_specs=ospecs,
        compiler_params=_cparams(("parallel", "parallel")),
        name="chunk_sgu",
    )(x, f, mods, mods, *consts)


def _route_kernel(lg_ref, rb_ref, slot_ref, wt_ref, cnt_ref, *, W):
    E, G = N_EXPERTS, N_GROUPS
    per = E // G
    s = _sigmoid(lg_ref[0])
    biased = s + rb_ref[:, 0:1]
    eidx = lax.broadcasted_iota(jnp.int32, (E, W), 0)

    best = None
    for gi in range(G):
        a, b, c, d = (biased[per * gi + i:per * gi + i + 1, :] for i in range(per))
        hi1, lo1, hi2, lo2 = jnp.maximum(a, b), jnp.minimum(a, b), jnp.maximum(c, d), jnp.minimum(c, d)
        top2 = jnp.maximum(hi1, hi2) + jnp.maximum(jnp.minimum(hi1, hi2), jnp.maximum(lo1, lo2))
        if best is None:
            best, bg = top2, jnp.zeros((1, W), jnp.int32)
        else:
            upd = top2 > best
            bg = jnp.where(upd, gi, bg)
            best = jnp.where(upd, top2, best)
    neg = -jnp.inf
    m1 = jnp.where((eidx // per) == bg, biased, neg)
    i1 = jnp.min(jnp.where(m1 == jnp.max(m1, axis=0, keepdims=True), eidx, E), axis=0, keepdims=True)
    sel1 = eidx == i1
    m2 = jnp.where(sel1, neg, m1)
    i2 = jnp.min(jnp.where(m2 == jnp.max(m2, axis=0, keepdims=True), eidx, E), axis=0, keepdims=True)
    sel2 = eidx == i2
    w1 = jnp.sum(jnp.where(sel1, s, 0.0), axis=0, keepdims=True)
    w2 = jnp.sum(jnp.where(sel2, s, 0.0), axis=0, keepdims=True)
    wt_ref[0] = jnp.concatenate([w1 / (w1 + w2), w2 / (w1 + w2)], axis=0)

    onehot = jnp.where(sel1 | sel2, 1.0, 0.0)
    blk = MOE_BLOCK
    upper = jnp.where(lax.broadcasted_iota(jnp.int32, (blk, blk), 0) < lax.broadcasted_iota(jnp.int32, (blk, blk), 1),
                      1.0, 0.0).astype(BF16)
    slots, sizes = [], []
    for i in range(W // blk):
        cols = slice(i * blk, (i + 1) * blk)
        ob = onehot[:, cols]
        rank = jnp.dot(ob.astype(BF16), upper, preferred_element_type=F32)
        size = jnp.ceil(jnp.sum(ob, axis=1, keepdims=True) * (1.0 / MOE_G)) * MOE_G
        offs, cum = [], jnp.zeros((1, 1), F32)
        for e in range(E):
            offs.append(cum)
            cum = cum + size[e:e + 1, :]
        pos = jnp.concatenate(offs, axis=0) + rank
        slots.append(jnp.concatenate([jnp.sum(jnp.where(sel1[:, cols], pos, 0.0), axis=0, keepdims=True),
                                      jnp.sum(jnp.where(sel2[:, cols], pos, 0.0), axis=0, keepdims=True)], axis=0))
        sizes.append(size)
    slot_ref[0] = jnp.concatenate(slots, axis=1).astype(jnp.int32)
    sizes.append(jnp.zeros((E, 128 - len(sizes)), F32))
    cnt_ref[0] = jnp.concatenate(sizes, axis=1).astype(jnp.int32)


def _route_call(logits, rbias):
    B, _, W = logits.shape
    win = lambda w: (w, 0, 0)
    return pl.pallas_call(
        functools.partial(_route_kernel, W=W),
        out_shape=[jax.ShapeDtypeStruct((B, 2, W), jnp.int32), jax.ShapeDtypeStruct((B, 2, W), F32),
                   jax.ShapeDtypeStruct((B, N_EXPERTS, 128), jnp.int32)],
        grid=(B,),
        in_specs=[pl.BlockSpec((1, N_EXPERTS, W), win), _full(rbias.shape)],
        out_specs=[pl.BlockSpec((1, 2, W), win), pl.BlockSpec((1, 2, W), win),
                   pl.BlockSpec((1, N_EXPERTS, 128), win)],
        compiler_params=_cparams(("parallel",)),
        name="moe_route",
    )(logits, rbias)


def _dispatch_tables(sizes, nb):
    max_rows = nb * MOE_CAP
    max_tiles = -(-max_rows // MOE_TM) + N_EXPERTS
    max_tiles = -(-max_tiles // MOE_TPS) * MOE_TPS
    rows = sizes.sum(-1)
    rows_p = -(-rows // MOE_TM) * MOE_TM
    end = jnp.cumsum(rows_p, axis=1)
    start = end - rows_p
    n_tiles = end[:, -1] // MOE_TM
    experts = jnp.arange(N_EXPERTS)
    last = jnp.max(jnp.where(rows > 0, experts[None, :], 0), axis=1, keepdims=True)

    t_row = jnp.arange(max_tiles) * MOE_TM
    owns = (start[:, :, None] <= t_row) & (t_row < end[:, :, None])
    texp = jnp.where(t_row[None, :] < end[:, -1:], jnp.sum(jnp.where(owns, experts[None, :, None], 0), axis=1), last)

    q_row = jnp.arange(max_tiles * MOE_TM // MOE_G) * MOE_G
    seg_start = start[:, :, None] + jnp.cumsum(sizes, axis=2) - sizes
    local = jnp.arange(nb)[None, None, :] * MOE_CAP + jnp.cumsum(sizes, axis=1) - sizes
    hit = (seg_start[..., None] <= q_row) & (q_row < (seg_start + sizes)[..., None])
    src_row = jnp.sum(jnp.where(hit, q_row + (local - seg_start)[..., None], 0), axis=(1, 2))
    table = jnp.where(jnp.any(hit, axis=(1, 2)), src_row // MOE_G, nb * MOE_CAP // MOE_G).astype(jnp.int32)
    return texp.astype(jnp.int32).reshape(-1), n_tiles.astype(jnp.int32), table[:, None, :], max_tiles


def _moe_kernel(texp_ref, nt_ref, h_ref, slot_ref, col_ref, tab_ref, *refs, nb):
    weights, (o_ref, rows_ref, xt_ref) = refs[:2 * MOE_TPS], refs[2 * MOE_TPS:]
    w = pl.program_id(0)
    t = pl.program_id(1)
    cap, gpt = MOE_CAP, MOE_TM // MOE_G
    zero_group = nb * cap // MOE_G

    @pl.when(t == 0)
    def _():
        rows_ref[nb * cap:nb * cap + 2 * MOE_G, :] = jnp.zeros((2 * MOE_G, D), rows_ref.dtype)
        rid = lax.broadcasted_iota(jnp.int32, (cap, MOE_BLOCK), 0)
        for b in range(nb):
            cols = slice(b * MOE_BLOCK, (b + 1) * MOE_BLOCK)
            hit = (slot_ref[0, 0:1, cols] == rid) | (slot_ref[0, 1:2, cols] == rid)
            disp = jnp.where(hit, 1.0, 0.0).astype(BF16)
            rows_ref[b * cap:(b + 1) * cap, :] = jnp.dot(
                disp, h_ref[0, cols, :], preferred_element_type=F32).astype(rows_ref.dtype)

    def expert_tile(tile, wgu_ref, wd_ref):
        src = [tab_ref[0, 0, tile * gpt + g] for g in range(gpt)]
        for g in range(gpt):
            xt_ref[g * MOE_G:(g + 1) * MOE_G, :] = rows_ref[pl.ds(pl.multiple_of(src[g] * MOE_G, MOE_G), MOE_G), :]
        gu = jnp.dot(xt_ref[...], wgu_ref[0], preferred_element_type=F32)
        gate, up = gu[:, :EXPERT_FFN], gu[:, EXPERT_FFN:]
        y = _dot(gate * _sigmoid(gate) * up, wd_ref[0]).astype(rows_ref.dtype)
        for g in range(gpt):
            dst = jnp.where(src[g] == zero_group, zero_group + 1, src[g])
            rows_ref[pl.ds(pl.multiple_of(dst * MOE_G, MOE_G), MOE_G), :] = y[g * MOE_G:(g + 1) * MOE_G, :]

    for i in range(MOE_TPS):
        tile = t * MOE_TPS + i
        pl.when(tile < nt_ref[w])(functools.partial(expert_tile, tile, weights[2 * i], weights[2 * i + 1]))

    @pl.when(t == pl.num_programs(1) - 1)
    def _():
        cid = lax.broadcasted_iota(jnp.int32, (MOE_BLOCK, cap), 1).astype(F32)
        for b in range(nb):
            col = col_ref[0, b * MOE_BLOCK:(b + 1) * MOE_BLOCK, :]
            comb = (jnp.where(col[:, 0:1] == cid, col[:, 2:3], 0.0)
                    + jnp.where(col[:, 1:2] == cid, col[:, 3:4], 0.0)).astype(BF16)
            o_ref[0, b * MOE_BLOCK:(b + 1) * MOE_BLOCK, :] = jnp.dot(
                comb, rows_ref[b * cap:(b + 1) * cap, :], preferred_element_type=F32).astype(o_ref.dtype)


def _moe_call(h2, slots, colinfo, texp, ntiles, table, max_tiles, wgu, wd):
    nw, W, _ = h2.shape
    nb = W // MOE_BLOCK
    win = lambda w, t, te, nt: (w, 0, 0)
    weight_specs = []
    for i in range(MOE_TPS):
        expert = functools.partial(lambda w, t, te, nt, i: (te[w * max_tiles + t * MOE_TPS + i], 0, 0), i=i)
        weight_specs += [pl.BlockSpec((1, D, 2 * EXPERT_FFN), expert), pl.BlockSpec((1, EXPERT_FFN, D), expert)]
    grid_spec = pltpu.PrefetchScalarGridSpec(
        num_scalar_prefetch=2, grid=(nw, max_tiles // MOE_TPS),
        in_specs=[pl.BlockSpec((1, W, D), win), pl.BlockSpec((1, 2, W), win), pl.BlockSpec((1, W, 4), win),
                  pl.BlockSpec((1, 1, table.shape[-1]), win, memory_space=pltpu.SMEM)] + weight_specs,
        out_specs=pl.BlockSpec((1, W, D), win),
        scratch_shapes=[pltpu.VMEM((nb * MOE_CAP + 2 * MOE_G, D), BF16), pltpu.VMEM((MOE_TM, D), BF16)])
    return pl.pallas_call(
        functools.partial(_moe_kernel, nb=nb),
        out_shape=jax.ShapeDtypeStruct((nw, W, D), BF16),
        grid_spec=grid_spec,
        compiler_params=_cparams(("parallel", "arbitrary")),
        name="moe_experts",
    )(texp, ntiles, h2, slots, colinfo, table, *([wgu, wd] * MOE_TPS))


def _moe_layer(h2, logits, rbias, wgu, wd):
    nb = h2.shape[1] // MOE_BLOCK
    slots, wts, sizes = _route_call(logits, rbias)
    texp, ntiles, table, max_tiles = _dispatch_tables(sizes[:, :, :nb], nb)
    colinfo = jnp.concatenate([slots.astype(F32), wts], axis=1).transpose(0, 2, 1)
    return _moe_call(h2, slots, colinfo, texp, ntiles, table, max_tiles, wgu, wd)


def _final_kernel(x, f, modp, g, o):
    o[0] = _rmsnorm(x[0] + modp[0, 0, 5:6, :] * f[0]) * g[...]


def _final_call(x, f, mods, layer, final_g):
    B, T, _ = x.shape
    tile = pl.BlockSpec((1, TM, D), lambda b, j: (b, j, 0))
    return pl.pallas_call(
        _final_kernel, out_shape=jax.ShapeDtypeStruct((B, T, D), F32), grid=(B, T // TM),
        in_specs=[tile, tile, _mod_spec(layer, B, False), _full(final_g.shape)], out_specs=tile,
        compiler_params=_cparams(("parallel", "parallel")),
        name="final_norm",
    )(x, f, mods, final_g)


def _rwkv_params(a, norm1_g, layer, rw_mu, rw_wr, rw_wk, rw_wv, rw_wo, rw_w0, rw_w1, rw_w2, rw_a0, rw_a1, rw_a2,
                 rw_g1, rw_g2, rw_kk, rw_ka, rw_rk, rw_lnx_g, rw_lnx_b, epair):
    lora = rw_w1.shape[-1]

    def block_diag(w2):
        z = jnp.zeros((lora, D), w2.dtype)
        return jnp.concatenate([jnp.concatenate([w2[0], z], axis=1), jnp.concatenate([z, w2[1]], axis=1)], axis=0)

    return dict(
        n1g=norm1_g[layer].reshape(1, D), mu=rw_mu[a],
        wr=rw_wr[a].astype(BF16), wk=rw_wk[a].astype(BF16), wv=rw_wv[a].astype(BF16), wo=rw_wo[a].astype(BF16),
        w1c=jnp.concatenate([rw_w1[a, 0], rw_w1[a, 1]], axis=1).astype(BF16), w2b=block_diag(rw_w2[a]).astype(BF16),
        w0=rw_w0[a],
        a1c=jnp.concatenate([rw_a1[a, 0], rw_a1[a, 1]], axis=1).astype(BF16), a2b=block_diag(rw_a2[a]).astype(BF16),
        a0=rw_a0[a],
        g1=rw_g1[a].astype(BF16), g2=rw_g2[a].astype(BF16),
        kk=rw_kk[a].reshape(1, D), ka=rw_ka[a].reshape(1, D), rk=rw_rk[a].reshape(1, D),
        lng=rw_lnx_g[a].reshape(1, D), lnb=rw_lnx_b[a].reshape(1, D), epair=epair)


def kernel(x, c, ctx, c_ctx, w_mod, b_mod, norm1_g, norm2_g, rw_mu, rw_wr, rw_wk, rw_wv, rw_wo, rw_w0, rw_w1, rw_w2, rw_a0, rw_a1, rw_a2, rw_v0, rw_v1, rw_v2, rw_g1, rw_g2, rw_kk, rw_ka, rw_rk, rw_lnx_g, rw_lnx_b, sg_w_in, sg_b_in, sg_ln_g, sg_ln_b, sg_w_s, sg_b_s, sg_w_out, w_router, router_bias, ex_w_gate, ex_w_up, ex_w_down, final_g):
    B, L, _ = x.shape
    Lc = ctx.shape[1]
    assert Lc == TM and L % TM == 0 and B + 1 <= MOD_ROWS and w_mod.shape[0] == 4

    cvec = jnp.concatenate([c, c_ctx[None, :], jnp.zeros((MOD_ROWS - B - 1, D), F32)], axis=0)
    mods = _mod_call(cvec, w_mod, b_mod).reshape(w_mod.shape[0], MOD_ROWS, N_MOD, D)
    xs = jnp.concatenate([ctx, x], axis=1)

    lane = jnp.arange(128) // HEAD
    epair = (lane[:, None] == lane[None, :]).astype(BF16)
    wrt = w_router.T
    rbias = jnp.broadcast_to(router_bias[:, None], (N_EXPERTS, 128))
    n2g = [norm2_g[i].reshape(1, D) for i in range(4)]
    wgu = [jnp.concatenate([ex_w_gate[i], ex_w_up[i]], axis=-1).astype(BF16) for i in range(4)]
    wd = [ex_w_down[i].astype(BF16) for i in range(4)]
    rw_common = (rw_mu, rw_wr, rw_wk, rw_wv, rw_wo, rw_w0, rw_w1, rw_w2, rw_a0, rw_a1, rw_a2, rw_g1, rw_g2,
                 rw_kk, rw_ka, rw_rk, rw_lnx_g, rw_lnx_b, epair)

    def sgu_params(b, layer):
        return dict(n1g=norm1_g[layer].reshape(1, D), win=sg_w_in[b].astype(BF16), bin=sg_b_in[b].reshape(1, -1),
                    lng=sg_ln_g[b].reshape(1, -1), lnb=sg_ln_b[b].reshape(1, -1), ws=sg_w_s[b].astype(BF16),
                    bs=jnp.broadcast_to(sg_b_s[b][:, :, None], (SG_GROUPS, SG_CHUNK, SG_HALF // SG_GROUPS)),
                    wout=sg_w_out[b].astype(BF16))

    p0 = _rwkv_params(0, norm1_g, 0, *rw_common)
    k0, kap0, v0, r0, a0, lam0, g0, bon0 = _rwkv_pre_call(xs, None, mods, 0, p0, None, None)
    y0 = _wkv_call(k0, kap0, v0, r0, a0, lam0, p0["ka"].reshape(PAIRS, 1, 128), Lc)
    x1, h1, lg1 = _rwkv_post_call(y0, bon0, g0, xs, mods, 0, p0, n2g[0], wrt, True)
    f0 = _moe_layer(h1, lg1, rbias, wgu[0], wd[0])

    x2, h2, lg2 = _sgu_call(x1, f0, mods, 1, sgu_params(0, 1), n2g[1], wrt, True)
    f1 = _moe_layer(h2, lg2, rbias, wgu[1], wd[1])

    p2 = _rwkv_params(1, norm1_g, 2, *rw_common)
    vres = dict(v0=rw_v0[0].reshape(1, D),
                v1=jnp.pad(rw_v1[0], ((0, 0), (0, 128 - rw_v1.shape[-1]))).astype(BF16),
                v2=jnp.pad(rw_v2[0], ((0, 128 - rw_v2.shape[-2]), (0, 0))).astype(BF16))
    k2, kap2, v2, r2, a2, lam2, g2, bon2, x2r = _rwkv_pre_call(x2, f1, mods, 2, p2, vres, v0)
    y2 = _wkv_call(k2, kap2, v2, r2, a2, lam2, p2["ka"].reshape(PAIRS, 1, 128), Lc)
    x3, h3, lg3 = _rwkv_post_call(y2, bon2, g2, x2r, mods, 2, p2, n2g[2], wrt, False)
    f2 = _moe_layer(h3, lg3, rbias, wgu[2], wd[2])

    x4, h4, lg4 = _sgu_call(x3, f2, mods, 3, sgu_params(1, 3), n2g[3], wrt, False)
    f3 = _moe_layer(h4, lg4, rbias, wgu[3], wd[3])
    return _final_call(x4, f3, mods, 3, final_g.reshape(1, D))
```

```python
import functools

import jax
import jax.numpy as jnp
from jax import lax
from jax.experimental import pallas as pl
from jax.experimental.pallas import tpu as pltpu

F32 = jnp.float32
BF16 = jnp.bfloat16

D = 1024
HEAD = 64
PAIRS = D // 128
GRID_W = 64
N_MOD = 6
N_EXPERTS = 16
N_GROUPS = 4
EXPERT_FFN = 512
SG_CHUNK = 128
SG_GROUPS = 16
SG_HALF = 2048
NORM_EPS = 1e-6
LN_EPS = 1e-5
GN_EPS = 64e-5
EXP_NEG_HALF = 0.6065306597126334

TM = 256
HALO = GRID_W
WKV_C = 64
WKV_GROUP = 8
WKV_SUB = 4
MOE_TM = 256
MOE_TPS = 2
MOE_BLOCK = 256
MOE_G = 16
MOE_CAP = 2 * MOE_BLOCK + N_EXPERTS * (MOE_G - 1)
MOD_ROWS = 40
VMEM_LIMIT = 56 * 1024 * 1024


def _cparams(sem):
    return pltpu.CompilerParams(dimension_semantics=sem, vmem_limit_bytes=VMEM_LIMIT)


def _dot(a, b):
    return jnp.dot(a.astype(BF16), b.astype(BF16), preferred_element_type=F32)


def _dot_nt(a, b):
    return lax.dot_general(a.astype(BF16), b.astype(BF16), (((1,), (1,)), ((), ())),
                           preferred_element_type=F32)


def _dot_tn(a, b):
    return lax.dot_general(a.astype(BF16), b.astype(BF16), (((0,), (0,)), ((), ())),
                           preferred_element_type=F32)


def _sigmoid(x):
    return 1.0 / (1.0 + jnp.exp(-x))


def _rmsnorm(x):
    return x * lax.rsqrt(jnp.mean(x * x, axis=-1, keepdims=True) + NORM_EPS)


def _adanorm(x, g, shift, scale):
    return _rmsnorm(x) * g * (1.0 + scale) + shift


def _full(shape):
    zeros = (0,) * len(shape)
    return pl.BlockSpec(shape, lambda *_: zeros)


def _mod_kernel(c_ref, w_ref, b_ref, o_ref):
    c = c_ref[...]
    o_ref[0] = jnp.dot(c * _sigmoid(c), w_ref[0], preferred_element_type=F32) + b_ref[0]


def _mod_call(cvec, w_mod, b_mod):
    depth = w_mod.shape[0]
    return pl.pallas_call(
        _mod_kernel,
        out_shape=jax.ShapeDtypeStruct((depth, MOD_ROWS, N_MOD * D), F32),
        grid=(depth, N_MOD),
        in_specs=[pl.BlockSpec((MOD_ROWS, D), lambda i, n: (0, 0)),
                  pl.BlockSpec((1, D, D), lambda i, n: (i, 0, n)),
                  pl.BlockSpec((1, 1, D), lambda i, n: (i, 0, n))],
        out_specs=pl.BlockSpec((1, MOD_ROWS, D), lambda i, n: (i, 0, n)),
        compiler_params=_cparams(("parallel", "parallel")),
        name="modulation",
    )(cvec, w_mod, b_mod.reshape(depth, 1, N_MOD * D))


def _mod_spec(layer, batch, ctx_tile):
    if ctx_tile:
        return pl.BlockSpec((1, 1, N_MOD, D), lambda b, j: (layer, jnp.where(j == 0, batch, b), 0, 0))
    return pl.BlockSpec((1, 1, N_MOD, D), lambda b, j: (layer, b, 0, 0))


def _rwkv_pre_kernel(*refs, has_f, has_vres, n_tiles):
    it = iter(refs)
    xm, xp, xn = next(it), next(it), next(it)
    if has_f:
        fm, fp, fn, modp = next(it), next(it), next(it), next(it)
    modc, n1g, mu = next(it), next(it), next(it)
    wr, wk, wv = next(it), next(it), next(it)
    w1c, w2b, w0 = next(it), next(it), next(it)
    a1c, a2b, a0 = next(it), next(it), next(it)
    g1, g2 = next(it), next(it)
    if has_vres:
        v0, v1, v2, vf = next(it), next(it), next(it), next(it)
    kkv, kav, rkv, ep = next(it), next(it), next(it), next(it)
    k_o, kap_o, v_o, r_o, a_o, lam_o, g_o, bon_o = (next(it) for _ in range(8))
    if has_f:
        x_o = next(it)

    j = pl.program_id(1)
    x_all = jnp.concatenate([xp[0], xm[0], xn[0]], axis=0)
    if has_f:
        f_all = jnp.concatenate([fp[0], fm[0], fn[0]], axis=0)
        x_all = x_all + modp[0, 0, 5:6, :] * f_all
        x_o[0] = x_all[HALO:HALO + TM]
    m = modc[0, 0]
    h_all = _adanorm(x_all, n1g[...], m[0:1], m[1:2])
    hm = h_all[HALO:HALO + TM]

    q = D // 4
    r = lax.broadcasted_iota(jnp.int32, (TM, q), 0)
    jv = jnp.zeros((TM, q), jnp.int32) + j
    is_ctx = jv == 0
    colp = r % GRID_W
    prev = pltpu.roll(hm, 1, axis=0)
    nxt = pltpu.roll(hm, TM - 1, axis=0)
    up = h_all[0:TM]
    down = h_all[2 * HALO:2 * HALO + TM]
    first, last = r == 0, r == TM - 1
    q0 = jnp.where((is_ctx & first) | (~is_ctx & (colp == 0)), 0.0, prev[:, 0:q])
    q1 = jnp.where(is_ctx, jnp.where(first, 0.0, prev[:, q:2 * q]),
                   jnp.where(colp == GRID_W - 1, 0.0, nxt[:, q:2 * q]))
    q2 = jnp.where(is_ctx, jnp.where(last, 0.0, nxt[:, 2 * q:3 * q]),
                   jnp.where((jv == 1) & (r < HALO), 0.0, up[:, 2 * q:3 * q]))
    q3 = jnp.where(is_ctx, jnp.where(last, 0.0, nxt[:, 3 * q:]),
                   jnp.where((jv == n_tiles - 1) & (r >= TM - HALO), 0.0, down[:, 3 * q:]))
    xx = jnp.concatenate([q0, q1, q2, q3], axis=1) - hm

    hm_b, xx_b, mu_b = hm.astype(BF16), xx.astype(BF16), mu[...].astype(BF16)

    def mix(i):
        return hm_b + xx_b * mu_b[i:i + 1, :]

    xr, xw, xk, xv, xa, xg = (mix(i) for i in range(6))
    zw = _dot(jnp.tanh(_dot(xw, w1c[...])), w2b[...])
    za = _dot(_dot(xa, a1c[...]), a2b[...])
    rr = _dot(xr, wr[...])
    kk = _dot(xk, wk[...])
    vv = _dot(xv, wv[...])
    if has_vres:
        gate = _sigmoid(v0[...] + _dot(_dot(xv, v1[...]), v2[...]))
        vfirst = jnp.concatenate([vf[0, p].astype(F32) for p in range(PAIRS)], axis=1)
        vv = vv + (vfirst - vv) * gate
    g_o[0] = _dot(_sigmoid(_dot(xg, g1[...])), g2[...]).astype(g_o.dtype)

    ka = kav[...]
    kd_sum = jnp.zeros((TM, D), F32)
    for d in range(2):
        z = w0[d:d + 1, :] + zw[:, d * D:(d + 1) * D]
        lam = -EXP_NEG_HALF * _sigmoid(z)
        a = _sigmoid(a0[d:d + 1, :] + za[:, d * D:(d + 1) * D])
        kd_sum = kd_sum + kk * (1.0 + (a - 1.0) * ka)
        for p in range(PAIRS):
            sl = slice(128 * p, 128 * (p + 1))
            lam_o[d, 0, p] = lam[:, sl]
            a_o[d, 0, p] = a[:, sl].astype(a_o.dtype)

    kkk = kk * kkv[...]
    rk = rr * rkv[...] * kd_sum
    for p in range(PAIRS):
        sl = slice(128 * p, 128 * (p + 1))
        kp = kkk[:, sl]
        ss = _dot(kp * kp, ep[...])
        kap_o[0, p] = (kp * lax.rsqrt(jnp.maximum(ss, 1e-24))).astype(kap_o.dtype)
        bon_o[0, :, sl] = (_dot(rk[:, sl], ep[...]) * vv[:, sl]).astype(bon_o.dtype)
        k_o[0, p] = kk[:, sl].astype(k_o.dtype)
        v_o[0, p] = vv[:, sl].astype(v_o.dtype)
        r_o[0, p] = rr[:, sl].astype(r_o.dtype)


def _rwkv_pre_call(xs, f_prev, mods, layer, prm, vres, v_first):
    B, T, _ = xs.shape
    nt = T // TM
    nh = T // HALO
    has_f = f_prev is not None
    has_vres = vres is not None
    main = pl.BlockSpec((1, TM, D), lambda b, j: (b, j, 0))
    prev = pl.BlockSpec((1, HALO, D), lambda b, j: (b, jnp.maximum(j * (TM // HALO) - 1, 0), 0))
    nxt = pl.BlockSpec((1, HALO, D), lambda b, j: (b, jnp.minimum((j + 1) * (TM // HALO), nh - 1), 0))
    pm = pl.BlockSpec((1, PAIRS, TM, 128), lambda b, j: (b, 0, j, 0))
    pm2 = pl.BlockSpec((2, 1, PAIRS, TM, 128), lambda b, j: (0, b, 0, j, 0))

    args, specs = [xs, xs, xs], [main, prev, nxt]
    if has_f:
        args += [f_prev, f_prev, f_prev, mods]
        specs += [main, prev, nxt, _mod_spec(layer - 1, B, True)]
    args += [mods, prm["n1g"], prm["mu"], prm["wr"], prm["wk"], prm["wv"], prm["w1c"], prm["w2b"], prm["w0"],
             prm["a1c"], prm["a2b"], prm["a0"], prm["g1"], prm["g2"]]
    specs += [_mod_spec(layer, B, True)] + [_full(a.shape) for a in args[len(specs) + 1:]]
    if has_vres:
        args += [vres["v0"], vres["v1"], vres["v2"], v_first]
        specs += [_full(vres["v0"].shape), _full(vres["v1"].shape), _full(vres["v2"].shape), pm]
    tail = [prm["kk"], prm["ka"], prm["rk"], prm["epair"]]
    args += tail
    specs += [_full(a.shape) for a in tail]

    pm_shape = jax.ShapeDtypeStruct((B, PAIRS, T, 128), BF16)
    out_shape = [pm_shape, pm_shape, pm_shape, pm_shape,
                 jax.ShapeDtypeStruct((2, B, PAIRS, T, 128), BF16),
                 jax.ShapeDtypeStruct((2, B, PAIRS, T, 128), F32),
                 jax.ShapeDtypeStruct((B, T, D), BF16),
                 jax.ShapeDtypeStruct((B, T, D), BF16)]
    out_specs = [pm, pm, pm, pm, pm2, pm2, main, main]
    if has_f:
        out_shape.append(jax.ShapeDtypeStruct((B, T, D), F32))
        out_specs.append(main)
    return pl.pallas_call(
        functools.partial(_rwkv_pre_kernel, has_f=has_f, has_vres=has_vres, n_tiles=nt),
        out_shape=out_shape, grid=(B, nt), in_specs=specs, out_specs=out_specs,
        compiler_params=_cparams(("parallel", "parallel")),
        name="rwkv_pre",
    )(*args)


def _wkv_kernel(k_ref, kap_ref, v_ref, r_ref, a_ref, lam_ref, ka_ref, y_ref, s_ref):
    C = WKV_C
    d = pl.program_id(1)
    j = pl.program_id(2)

    @pl.when(j == 0)
    def _():
        s_ref[...] = jnp.zeros_like(s_ref)

    dv = jnp.zeros((C, 128), jnp.int32) + d
    tr = lax.broadcasted_iota(jnp.int32, (C, 128), 0)
    lane = lax.broadcasted_iota(jnp.int32, (C, 128), 1)
    tc = lane % C
    strict = ((dv == 0) & (tc < tr)) | ((dv != 0) & (tc > tr))
    incl = strict | (tc == tr)
    eye_ld = tc == tr
    cum_mask = jnp.where(incl[:, :C], 1.0, 0.0).astype(BF16)
    lane_lo = lane < HEAD
    row2 = lax.broadcasted_iota(jnp.int32, (128, 128), 0)
    col2 = lax.broadcasted_iota(jnp.int32, (128, 128), 1)
    same = (row2 // HEAD) == (col2 // HEAD)
    eye = row2 == col2

    def stack(x):
        z = jnp.zeros_like(x)
        return jnp.concatenate([jnp.where(lane_lo, x, z), jnp.where(lane_lo, z, x)], axis=0)

    offs = [pl.multiple_of(jnp.where(d == 0, q * C, (WKV_SUB - 1 - q) * C), C) for q in range(WKV_SUB)]

    def group(ps):
        def each(fn, *lists):
            return [fn(*xs) for xs in zip(*lists)]

        inst = [(p, q) for q in range(WKV_SUB) for p in ps]
        rows = [pl.ds(offs[q], C) for _, q in inst]
        lam = [lam_ref[0, 0, p, rw, :] for (p, _), rw in zip(inst, rows)]
        kap = [kap_ref[0, p, rw, :].astype(F32) for (p, _), rw in zip(inst, rows)]
        v = [v_ref[0, p, rw, :] for (p, _), rw in zip(inst, rows)]
        v_z = [stack(x) for x in v]
        a = [a_ref[0, 0, p, rw, :].astype(F32) for (p, _), rw in zip(inst, rows)]
        kd = [k_ref[0, p, rw, :].astype(F32) * (1.0 + (ai - 1.0) * ka_ref[p])
              for (p, _), rw, ai in zip(inst, rows, a)]
        bet = each(lambda x, y: x * y, kap, a)

        def prefix(l):
            hi = l.astype(BF16)
            lo = (l - hi.astype(F32)).astype(BF16)
            cs = jnp.dot(cum_mask, jnp.concatenate([hi, lo], axis=1), preferred_element_type=F32)
            return cs[:, :128] + cs[:, 128:]

        lc = each(prefix, lam)
        ltot = [jnp.sum(l, axis=0, keepdims=True) for l in lam]
        e_neg = [jnp.exp(-x) for x in lc]
        qk = each(lambda kp, c, l: (kp * jnp.exp(c - l)).astype(BF16), kap, lc, lam)
        rt = [r_ref[0, p, rw, :].astype(F32) * jnp.exp(c) for (p, _), rw, c in zip(inst, rows, lc)]

        def scores(q, ri, kdi, bi, en):
            kt = (kdi * en).astype(BF16)
            bt = (bi * en).astype(BF16)
            return _dot_nt(jnp.concatenate([q, ri.astype(BF16)], axis=0),
                           jnp.concatenate([stack(kt), stack(bt)], axis=0))

        sc = each(scores, qk, rt, kd, bet, e_neg)
        m_k = [jnp.where(strict, s[:C, :128], 0.0).astype(BF16) for s in sc]
        a_kb = [jnp.concatenate([jnp.where(incl, s[C:, :128], 0.0), jnp.where(incl, -s[C:, 128:], 0.0)],
                                axis=1).astype(BF16) for s in sc]
        mkv = each(_dot, m_k, v_z)

        levels = C.bit_length() - 1
        pw = [jnp.where(strict, -s[:C, 128:], 0.0).astype(BF16) for s in sc]
        t_inv = [jnp.where(eye_ld, 1.0, 0.0) + x.astype(F32) for x in pw]
        pw = [_dot(x, stack(x)).astype(BF16) for x in pw]
        for _ in range(1, levels - 1):
            both = each(lambda x, t: _dot(x, jnp.concatenate([stack(x), stack(t.astype(BF16))], axis=1)), pw, t_inv)
            pw = [x[:, :128].astype(BF16) for x in both]
            t_inv = each(lambda t, x: t + x[:, 128:], t_inv, both)
        t_inv = each(lambda t, x: t + _dot(x, stack(t.astype(BF16))), t_inv, pw)

        tw = each(lambda t, q, m: _dot(t, jnp.concatenate([stack(q), stack(m.astype(BF16))], axis=1)).astype(BF16),
                  t_inv, qk, mkv)
        yr = each(lambda ab, vz, t: _dot(ab, jnp.concatenate(
            [jnp.concatenate([vz, jnp.zeros_like(vz)], axis=1),
             jnp.concatenate([stack(t[:, 128:]), stack(t[:, :128])], axis=1)], axis=0)), a_kb, v_z, tw)
        e_tail = each(lambda lt, c: jnp.exp(lt - c), ltot, lc)
        hg = each(lambda x, b, e, vi, t: _dot_tn(
            jnp.concatenate([x * e, -b * e], axis=0),
            jnp.concatenate([jnp.concatenate([vi, jnp.zeros_like(vi)], axis=1),
                             jnp.concatenate([t[:, 128:], t[:, :128]], axis=1)], axis=0)), kd, bet, e_tail, v, tw)
        rhat = each(lambda ri, x: ri + x[:, 128:], rt, yr)
        g = each(lambda lt, x: jnp.where(eye, jnp.exp(lt), 0.0) + jnp.where(same, x[:, 128:], 0.0), ltot, hg)

        def advance(s0, gi, rh):
            s_hi = s0.astype(BF16)
            s_lo = (s0 - s_hi.astype(F32)).astype(BF16)
            gs = _dot(jnp.concatenate([gi, rh], axis=0), jnp.concatenate([s_hi, s_lo], axis=1))
            return gs[:, :128] + gs[:, 128:]

        state = [s_ref[p] for p in ps]
        n = len(ps)
        for q in range(WKV_SUB):
            sl = slice(q * n, (q + 1) * n)
            gs = each(advance, state, g[sl], rhat[sl])
            state = each(lambda x, h: x[:128] + jnp.where(same, h[:, :128], 0.0), gs, hg[sl])
            for p, x, yl in zip(ps, gs, yr[sl]):
                y_ref[0, 0, p, pl.ds(offs[q], C), :] = (x[128:] + yl[:, :128]).astype(y_ref.dtype)
        for p, x in zip(ps, state):
            s_ref[p] = x

    for first in range(0, PAIRS, WKV_GROUP):
        group(range(first, first + WKV_GROUP))


def _wkv_call(k, kap, v, r, a, lam, ka, ctx_len, out_dtype=BF16):
    B, _, T, _ = k.shape
    C = WKV_C * WKV_SUB
    assert T % C == 0 and ctx_len % C == 0
    ns = T // C
    nc = ctx_len // C

    def chunk(d, j):
        rev = jnp.where(j < nc, nc - 1 - j, ns - 1 + nc - j)
        return jnp.where(d == 0, j, rev)

    tok = pl.BlockSpec((1, PAIRS, C, 128), lambda b, d, j: (b, 0, chunk(d, j), 0))
    per_dir = pl.BlockSpec((1, 1, PAIRS, C, 128), lambda b, d, j: (d, b, 0, chunk(d, j), 0))
    return pl.pallas_call(
        _wkv_kernel,
        out_shape=jax.ShapeDtypeStruct((2, B, PAIRS, T, 128), out_dtype),
        grid=(B, 2, ns),
        in_specs=[tok, tok, tok, tok, per_dir, per_dir, _full((PAIRS, 1, 128))],
        out_specs=per_dir,
        scratch_shapes=[pltpu.VMEM((PAIRS, 128, 128), F32)],
        compiler_params=_cparams(("parallel", "parallel", "arbitrary")),
        name="wkv_chunked",
    )(k, kap, v, r, a, lam, ka)


def _mixer_tail(x, mix_out, m, n2g, wrt, x_o, h_o, lg_o):
    x_new = x + m[2:3] * mix_out
    h2 = _adanorm(x_new, n2g[...], m[3:4], m[4:5])
    x_o[0] = x_new
    h_o[0] = h2.astype(h_o.dtype)
    lg_o[0] = lax.dot_general(wrt[...], h2, (((1,), (1,)), ((), ())), preferred_element_type=F32)


def _tail_out(B, nt):
    t_out = nt * TM
    shapes = [jax.ShapeDtypeStruct((B, t_out, D), F32), jax.ShapeDtypeStruct((B, t_out, D), BF16),
              jax.ShapeDtypeStruct((B, N_EXPERTS, t_out), F32)]
    tile = pl.BlockSpec((1, TM, D), lambda b, j: (b, j, 0))
    specs = [tile, tile, pl.BlockSpec((1, N_EXPERTS, TM), lambda b, j: (b, 0, j))]
    return shapes, specs


def _rwkv_post_kernel(y0, y1, bon, g, x, modc, lng, lnb, wo, n2g, wrt, ep, x_o, h_o, lg_o):
    parts = []
    for p in range(PAIRS):
        y = y0[0, 0, p].astype(F32) + y1[0, 0, p].astype(F32)
        dlt = y - _dot(y, ep[...]) * (1.0 / HEAD)
        var = _dot(dlt * dlt, ep[...]) * (1.0 / HEAD)
        parts.append(dlt * lax.rsqrt(var + GN_EPS))
    yn = jnp.concatenate(parts, axis=1)
    z = (yn * lng[...] + lnb[...] + bon[0].astype(F32)) * g[0].astype(F32)
    _mixer_tail(x[0], _dot(z, wo[...]), modc[0, 0], n2g, wrt, x_o, h_o, lg_o)


def _rwkv_post_call(y, bonus, g, x, mods, layer, prm, n2g, wrt, ctx_out):
    B, T, _ = x.shape
    nt = T // TM
    off = 0 if ctx_out else 1
    tile = pl.BlockSpec((1, TM, D), lambda b, j: (b, j + off, 0))
    ydir = [pl.BlockSpec((1, 1, PAIRS, TM, 128), functools.partial(lambda b, j, d: (d, b, 0, j + off, 0), d=d))
            for d in range(2)]
    consts = [prm["lng"], prm["lnb"], prm["wo"], n2g, wrt, prm["epair"]]
    shapes, ospecs = _tail_out(B, nt - off)
    return pl.pallas_call(
        _rwkv_post_kernel, out_shape=shapes, grid=(B, nt - off),
        in_specs=ydir + [tile, tile, tile, _mod_spec(layer, B, ctx_out)] + [_full(a.shape) for a in consts],
        out_specs=ospecs,
        compiler_params=_cparams(("parallel", "parallel")),
        name="rwkv_post",
    )(y, y, bonus, g, x, mods, *consts)


def _sgu_kernel(x, f, modp, modc, n1g, win, bin_, lng, lnb, ws, bs, wout, n2g, wrt, x_o, h_o, lg_o):
    x1 = x[0] + modp[0, 0, 5:6, :] * f[0]
    m = modc[0, 0]
    h = _adanorm(x1, n1g[...], m[0:1], m[1:2])
    hb = h.astype(BF16)

    def gelu(z):
        return 0.5 * z * (1.0 + lax.erf(z * 0.7071067811865476))

    zv = jnp.dot(hb, win[:, SG_HALF:], preferred_element_type=F32) + bin_[:, SG_HALF:]
    zu = jnp.dot(hb, win[:, :SG_HALF], preferred_element_type=F32) + bin_[:, :SG_HALF]
    v = gelu(zv)
    mu = jnp.mean(v, axis=-1, keepdims=True)
    dv = v - mu
    var = jnp.mean(dv * dv, axis=-1, keepdims=True)
    vn = (dv * lax.rsqrt(var + LN_EPS) * lng[...] + lnb[...]).astype(BF16)
    gw = SG_HALF // SG_GROUPS
    chunks = [slice(c * SG_CHUNK, (c + 1) * SG_CHUNK) for c in range(TM // SG_CHUNK)]
    groups = [slice(gi * gw, (gi + 1) * gw) for gi in range(SG_GROUPS)]
    vs = [[jnp.dot(ws[gi], vn[rs, cs], preferred_element_type=F32) + bs[gi] for gi, cs in enumerate(groups)]
          for rs in chunks]
    u = gelu(zu)
    gated = jnp.concatenate(
        [jnp.concatenate([(u[rs, cs] * vs[c][gi]).astype(BF16) for gi, cs in enumerate(groups)], axis=1)
         for c, rs in enumerate(chunks)], axis=0)
    _mixer_tail(x1, _dot(gated, wout[...]), m, n2g, wrt, x_o, h_o, lg_o)


def _sgu_call(x, f, mods, layer, prm, n2g, wrt, has_ctx):
    B, T, _ = x.shape
    nt = T // TM
    tile = pl.BlockSpec((1, TM, D), lambda b, j: (b, j, 0))
    consts = [prm["n1g"], prm["win"], prm["bin"], prm["lng"], prm["lnb"], prm["ws"], prm["bs"], prm["wout"],
              n2g, wrt]
    shapes, ospecs = _tail_out(B, nt)
    return pl.pallas_call(
        _sgu_kernel, out_shape=shapes, grid=(B, nt),
        in_specs=[tile, tile, _mod_spec(layer - 1, B, has_ctx), _mod_spec(layer, B, has_ctx)]
        + [_full(a.shape) for a in consts],
        out_specs=ospecs,
        compiler_params=_cparams(("parallel", "parallel")),
        name="chunk_sgu",
    )(x, f, mods, mods, *consts)


def _route_kernel(lg_ref, rb_ref, slot_ref, wt_ref, cnt_ref, *, W):
    E, G = N_EXPERTS, N_GROUPS
    per = E // G
    s = _sigmoid(lg_ref[0])
    biased = s + rb_ref[:, 0:1]
    eidx = lax.broadcasted_iota(jnp.int32, (E, W), 0)

    best = None
    for gi in range(G):
        a, b, c, d = (biased[per * gi + i:per * gi + i + 1, :] for i in range(per))
        hi1, lo1, hi2, lo2 = jnp.maximum(a, b), jnp.minimum(a, b), jnp.maximum(c, d), jnp.minimum(c, d)
        top2 = jnp.maximum(hi1, hi2) + jnp.maximum(jnp.minimum(hi1, hi2), jnp.maximum(lo1, lo2))
        if best is None:
            best, bg = top2, jnp.zeros((1, W), jnp.int32)
        else:
            upd = top2 > best
            bg = jnp.where(upd, gi, bg)
            best = jnp.where(upd, top2, best)
    neg = -jnp.inf
    m1 = jnp.where((eidx // per) == bg, biased, neg)
    i1 = jnp.min(jnp.where(m1 == jnp.max(m1, axis=0, keepdims=True), eidx, E), axis=0, keepdims=True)
    sel1 = eidx == i1
    m2 = jnp.where(sel1, neg, m1)
    i2 = jnp.min(jnp.where(m2 == jnp.max(m2, axis=0, keepdims=True), eidx, E), axis=0, keepdims=True)
    sel2 = eidx == i2
    w1 = jnp.sum(jnp.where(sel1, s, 0.0), axis=0, keepdims=True)
    w2 = jnp.sum(jnp.where(sel2, s, 0.0), axis=0, keepdims=True)
    wt_ref[0] = jnp.concatenate([w1 / (w1 + w2), w2 / (w1 + w2)], axis=0)

    onehot = jnp.where(sel1 | sel2, 1.0, 0.0)
    blk = MOE_BLOCK
    upper = jnp.where(lax.broadcasted_iota(jnp.int32, (blk, blk), 0) < lax.broadcasted_iota(jnp.int32, (blk, blk), 1),
                      1.0, 0.0).astype(BF16)
    slots, sizes = [], []
    for i in range(W // blk):
        cols = slice(i * blk, (i + 1) * blk)
        ob = onehot[:, cols]
        rank = jnp.dot(ob.astype(BF16), upper, preferred_element_type=F32)
        size = jnp.ceil(jnp.sum(ob, axis=1, keepdims=True) * (1.0 / MOE_G)) * MOE_G
        offs, cum = [], jnp.zeros((1, 1), F32)
        for e in range(E):
            offs.append(cum)
            cum = cum + size[e:e + 1, :]
        pos = jnp.concatenate(offs, axis=0) + rank
        slots.append(jnp.concatenate([jnp.sum(jnp.where(sel1[:, cols], pos, 0.0), axis=0, keepdims=True),
                                      jnp.sum(jnp.where(sel2[:, cols], pos, 0.0), axis=0, keepdims=True)], axis=0))
        sizes.append(size)
    slot_ref[0] = jnp.concatenate(slots, axis=1).astype(jnp.int32)
    sizes.append(jnp.zeros((E, 128 - len(sizes)), F32))
    cnt_ref[0] = jnp.concatenate(sizes, axis=1).astype(jnp.int32)


def _route_call(logits, rbias):
    B, _, W = logits.shape
    win = lambda w: (w, 0, 0)
    return pl.pallas_call(
        functools.partial(_route_kernel, W=W),
        out_shape=[jax.ShapeDtypeStruct((B, 2, W), jnp.int32), jax.ShapeDtypeStruct((B, 2, W), F32),
                   jax.ShapeDtypeStruct((B, N_EXPERTS, 128), jnp.int32)],
        grid=(B,),
        in_specs=[pl.BlockSpec((1, N_EXPERTS, W), win), _full(rbias.shape)],
        out_specs=[pl.BlockSpec((1, 2, W), win), pl.BlockSpec((1, 2, W), win),
                   pl.BlockSpec((1, N_EXPERTS, 128), win)],
        compiler_params=_cparams(("parallel",)),
        name="moe_route",
    )(logits, rbias)


def _dispatch_tables(sizes, nb):
    max_rows = nb * MOE_CAP
    max_tiles = -(-max_rows // MOE_TM) + N_EXPERTS
    max_tiles = -(-max_tiles // MOE_TPS) * MOE_TPS
    rows = sizes.sum(-1)
    rows_p = -(-rows // MOE_TM) * MOE_TM
    end = jnp.cumsum(rows_p, axis=1)
    start = end - rows_p
    n_tiles = end[:, -1] // MOE_TM
    experts = jnp.arange(N_EXPERTS)
    last = jnp.max(jnp.where(rows > 0, experts[None, :], 0), axis=1, keepdims=True)

    t_row = jnp.arange(max_tiles) * MOE_TM
    owns = (start[:, :, None] <= t_row) & (t_row < end[:, :, None])
    texp = jnp.where(t_row[None, :] < end[:, -1:], jnp.sum(jnp.where(owns, experts[None, :, None], 0), axis=1), last)

    q_row = jnp.arange(max_tiles * MOE_TM // MOE_G) * MOE_G
    seg_start = start[:, :, None] + jnp.cumsum(sizes, axis=2) - sizes
    local = jnp.arange(nb)[None, None, :] * MOE_CAP + jnp.cumsum(sizes, axis=1) - sizes
    hit = (seg_start[..., None] <= q_row) & (q_row < (seg_start + sizes)[..., None])
    src_row = jnp.sum(jnp.where(hit, q_row + (local - seg_start)[..., None], 0), axis=(1, 2))
    table = jnp.where(jnp.any(hit, axis=(1, 2)), src_row // MOE_G, nb * MOE_CAP // MOE_G).astype(jnp.int32)
    return texp.astype(jnp.int32).reshape(-1), n_tiles.astype(jnp.int32), table[:, None, :], max_tiles


def _moe_kernel(texp_ref, nt_ref, h_ref, slot_ref, col_ref, tab_ref, *refs, nb):
    weights, (o_ref, rows_ref, xt_ref) = refs[:2 * MOE_TPS], refs[2 * MOE_TPS:]
    w = pl.program_id(0)
    t = pl.program_id(1)
    cap, gpt = MOE_CAP, MOE_TM // MOE_G
    zero_group = nb * cap // MOE_G

    @pl.when(t == 0)
    def _():
        rows_ref[nb * cap:nb * cap + 2 * MOE_G, :] = jnp.zeros((2 * MOE_G, D), rows_ref.dtype)
        rid = lax.broadcasted_iota(jnp.int32, (cap, MOE_BLOCK), 0)
        for b in range(nb):
            cols = slice(b * MOE_BLOCK, (b + 1) * MOE_BLOCK)
            hit = (slot_ref[0, 0:1, cols] == rid) | (slot_ref[0, 1:2, cols] == rid)
            disp = jnp.where(hit, 1.0, 0.0).astype(BF16)
            rows_ref[b * cap:(b + 1) * cap, :] = jnp.dot(
                disp, h_ref[0, cols, :], preferred_element_type=F32).astype(rows_ref.dtype)

    def expert_tile(tile, wgu_ref, wd_ref):
        src = [tab_ref[0, 0, tile * gpt + g] for g in range(gpt)]
        for g in range(gpt):
            xt_ref[g * MOE_G:(g + 1) * MOE_G, :] = rows_ref[pl.ds(pl.multiple_of(src[g] * MOE_G, MOE_G), MOE_G), :]
        gu = jnp.dot(xt_ref[...], wgu_ref[0], preferred_element_type=F32)
        gate, up = gu[:, :EXPERT_FFN], gu[:, EXPERT_FFN:]
        y = _dot(gate * _sigmoid(gate) * up, wd_ref[0]).astype(rows_ref.dtype)
        for g in range(gpt):
            dst = jnp.where(src[g] == zero_group, zero_group + 1, src[g])
            rows_ref[pl.ds(pl.multiple_of(dst * MOE_G, MOE_G), MOE_G), :] = y[g * MOE_G:(g + 1) * MOE_G, :]

    for i in range(MOE_TPS):
        tile = t * MOE_TPS + i
        pl.when(tile < nt_ref[w])(functools.partial(expert_tile, tile, weights[2 * i], weights[2 * i + 1]))

    @pl.when(t == pl.num_programs(1) - 1)
    def _():
        cid = lax.broadcasted_iota(jnp.int32, (MOE_BLOCK, cap), 1).astype(F32)
        for b in range(nb):
            col = col_ref[0, b * MOE_BLOCK:(b + 1) * MOE_BLOCK, :]
            comb = (jnp.where(col[:, 0:1] == cid, col[:, 2:3], 0.0)
                    + jnp.where(col[:, 1:2] == cid, col[:, 3:4], 0.0)).astype(BF16)
            o_ref[0, b * MOE_BLOCK:(b + 1) * MOE_BLOCK, :] = jnp.dot(
                comb, rows_ref[b * cap:(b + 1) * cap, :], preferred_element_type=F32).astype(o_ref.dtype)


def _moe_call(h2, slots, colinfo, texp, ntiles, table, max_tiles, wgu, wd):
    nw, W, _ = h2.shape
    nb = W // MOE_BLOCK
    win = lambda w, t, te, nt: (w, 0, 0)
    weight_specs = []
    for i in range(MOE_TPS):
        expert = functools.partial(lambda w, t, te, nt, i: (te[w * max_tiles + t * MOE_TPS + i], 0, 0), i=i)
        weight_specs += [pl.BlockSpec((1, D, 2 * EXPERT_FFN), expert), pl.BlockSpec((1, EXPERT_FFN, D), expert)]
    grid_spec = pltpu.PrefetchScalarGridSpec(
        num_scalar_prefetch=2, grid=(nw, max_tiles // MOE_TPS),
        in_specs=[pl.BlockSpec((1, W, D), win), pl.BlockSpec((1, 2, W), win), pl.BlockSpec((1, W, 4), win),
                  pl.BlockSpec((1, 1, table.shape[-1]), win, memory_space=pltpu.SMEM)] + weight_specs,
        out_specs=pl.BlockSpec((1, W, D), win),
        scratch_shapes=[pltpu.VMEM((nb * MOE_CAP + 2 * MOE_G, D), BF16), pltpu.VMEM((MOE_TM, D), BF16)])
    return pl.pallas_call(
        functools.partial(_moe_kernel, nb=nb),
        out_shape=jax.ShapeDtypeStruct((nw, W, D), BF16),
        grid_spec=grid_spec,
        compiler_params=_cparams(("parallel", "arbitrary")),
        name="moe_experts",
    )(texp, ntiles, h2, slots, colinfo, table, *([wgu, wd] * MOE_TPS))


def _moe_layer(h2, logits, rbias, wgu, wd):
    nb = h2.shape[1] // MOE_BLOCK
    slots, wts, sizes = _route_call(logits, rbias)
    texp, ntiles, table, max_tiles = _dispatch_tables(sizes[:, :, :nb], nb)
    colinfo = jnp.concatenate([slots.astype(F32), wts], axis=1).transpose(0, 2, 1)
    return _moe_call(h2, slots, colinfo, texp, ntiles, table, max_tiles, wgu, wd)


def _final_kernel(x, f, modp, g, o):
    o[0] = _rmsnorm(x[0] + modp[0, 0, 5:6, :] * f[0]) * g[...]


def _final_call(x, f, mods, layer, final_g):
    B, T, _ = x.shape
    tile = pl.BlockSpec((1, TM, D), lambda b, j: (b, j, 0))
    return pl.pallas_call(
        _final_kernel, out_shape=jax.ShapeDtypeStruct((B, T, D), F32), grid=(B, T // TM),
        in_specs=[tile, tile, _mod_spec(layer, B, False), _full(final_g.shape)], out_specs=tile,
        compiler_params=_cparams(("parallel", "parallel")),
        name="final_norm",
    )(x, f, mods, final_g)


def _rwkv_params(a, norm1_g, layer, rw_mu, rw_wr, rw_wk, rw_wv, rw_wo, rw_w0, rw_w1, rw_w2, rw_a0, rw_a1, rw_a2,
                 rw_g1, rw_g2, rw_kk, rw_ka, rw_rk, rw_lnx_g, rw_lnx_b, epair):
    lora = rw_w1.shape[-1]

    def block_diag(w2):
        z = jnp.zeros((lora, D), w2.dtype)
        return jnp.concatenate([jnp.concatenate([w2[0], z], axis=1), jnp.concatenate([z, w2[1]], axis=1)], axis=0)

    return dict(
        n1g=norm1_g[layer].reshape(1, D), mu=rw_mu[a],
        wr=rw_wr[a].astype(BF16), wk=rw_wk[a].astype(BF16), wv=rw_wv[a].astype(BF16), wo=rw_wo[a].astype(BF16),
        w1c=jnp.concatenate([rw_w1[a, 0], rw_w1[a, 1]], axis=1).astype(BF16), w2b=block_diag(rw_w2[a]).astype(BF16),
        w0=rw_w0[a],
        a1c=jnp.concatenate([rw_a1[a, 0], rw_a1[a, 1]], axis=1).astype(BF16), a2b=block_diag(rw_a2[a]).astype(BF16),
        a0=rw_a0[a],
        g1=rw_g1[a].astype(BF16), g2=rw_g2[a].astype(BF16),
        kk=rw_kk[a].reshape(1, D), ka=rw_ka[a].reshape(1, D), rk=rw_rk[a].reshape(1, D),
        lng=rw_lnx_g[a].reshape(1, D), lnb=rw_lnx_b[a].reshape(1, D), epair=epair)


def kernel(x, c, ctx, c_ctx, w_mod, b_mod, norm1_g, norm2_g, rw_mu, rw_wr, rw_wk, rw_wv, rw_wo, rw_w0, rw_w1, rw_w2, rw_a0, rw_a1, rw_a2, rw_v0, rw_v1, rw_v2, rw_g1, rw_g2, rw_kk, rw_ka, rw_rk, rw_lnx_g, rw_lnx_b, sg_w_in, sg_b_in, sg_ln_g, sg_ln_b, sg_w_s, sg_b_s, sg_w_out, w_router, router_bias, ex_w_gate, ex_w_up, ex_w_down, final_g):
    B, L, _ = x.shape
    Lc = ctx.shape[1]
    assert Lc == TM and L % TM == 0 and B + 1 <= MOD_ROWS and w_mod.shape[0] == 4

    cvec = jnp.concatenate([c, c_ctx[None, :], jnp.zeros((MOD_ROWS - B - 1, D), F32)], axis=0)
    mods = _mod_call(cvec, w_mod, b_mod).reshape(w_mod.shape[0], MOD_ROWS, N_MOD, D)
    xs = jnp.concatenate([ctx, x], axis=1)

    lane = jnp.arange(128) // HEAD
    epair = (lane[:, None] == lane[None, :]).astype(BF16)
    wrt = w_router.T
    rbias = jnp.broadcast_to(router_bias[:, None], (N_EXPERTS, 128))
    n2g = [norm2_g[i].reshape(1, D) for i in range(4)]
    wgu = [jnp.concatenate([ex_w_gate[i], ex_w_up[i]], axis=-1).astype(BF16) for i in range(4)]
    wd = [ex_w_down[i].astype(BF16) for i in range(4)]
    rw_common = (rw_mu, rw_wr, rw_wk, rw_wv, rw_wo, rw_w0, rw_w1, rw_w2, rw_a0, rw_a1, rw_a2, rw_g1, rw_g2,
                 rw_kk, rw_ka, rw_rk, rw_lnx_g, rw_lnx_b, epair)

    def sgu_params(b, layer):
        return dict(n1g=norm1_g[layer].reshape(1, D), win=sg_w_in[b].astype(BF16), bin=sg_b_in[b].reshape(1, -1),
                    lng=sg_ln_g[b].reshape(1, -1), lnb=sg_ln_b[b].reshape(1, -1), ws=sg_w_s[b].astype(BF16),
                    bs=jnp.broadcast_to(sg_b_s[b][:, :, None], (SG_GROUPS, SG_CHUNK, SG_HALF // SG_GROUPS)),
                    wout=sg_w_out[b].astype(BF16))

    p0 = _rwkv_params(0, norm1_g, 0, *rw_common)
    k0, kap0, v0, r0, a0, lam0, g0, bon0 = _rwkv_pre_call(xs, None, mods, 0, p0, None, None)
    y0 = _wkv_call(k0, kap0, v0, r0, a0, lam0, p0["ka"].reshape(PAIRS, 1, 128), Lc)
    x1, h1, lg1 = _rwkv_post_call(y0, bon0, g0, xs, mods, 0, p0, n2g[0], wrt, True)
    f0 = _moe_layer(h1, lg1, rbias, wgu[0], wd[0])

    x2, h2, lg2 = _sgu_call(x1, f0, mods, 1, sgu_params(0, 1), n2g[1], wrt, True)
    f1 = _moe_layer(h2, lg2, rbias, wgu[1], wd[1])

    p2 = _rwkv_params(1, norm1_g, 2, *rw_common)
    vres = dict(v0=rw_v0[0].reshape(1, D),
                v1=jnp.pad(rw_v1[0], ((0, 0), (0, 128 - rw_v1.shape[-1]))).astype(BF16),
                v2=jnp.pad(rw_v2[0], ((0, 128 - rw_v2.shape[-2]), (0, 0))).astype(BF16))
    k2, kap2, v2, r2, a2, lam2, g2, bon2, x2r = _rwkv_pre_call(x2, f1, mods, 2, p2, vres, v0)
    y2 = _wkv_call(k2, kap2, v2, r2, a2, lam2, p2["ka"].reshape(PAIRS, 1, 128), Lc)
    x3, h3, lg3 = _rwkv_post_call(y2, bon2, g2, x2r, mods, 2, p2, n2g[2], wrt, False)
    f2 = _moe_layer(h3, lg3, rbias, wgu[2], wd[2])

    x4, h4, lg4 = _sgu_call(x3, f2, mods, 3, sgu_params(1, 3), n2g[3], wrt, False)
    f3 = _moe_layer(h4, lg4, rbias, wgu[3], wd[3])
    return _final_call(x4, f3, mods, 3, final_g.reshape(1, D))
```
